```python
import math
import jax, jax.numpy as jnp
from jax import lax
import numpy as np

D_MODEL = 1024
BATCH = 2
SEQ = 8192
DEPTH = 4

N_MEM = 256
EPS = 1e-6

MLA_HEADS = 8
MLA_NOPE = 64
MLA_ROPE = 32
MLA_V = 64
MLA_Q_RANK = 256
MLA_KV_RANK = 128
ROPE_THETA = 10000.0
Q_BLOCK = 128

SSM_HEADS = 4
SSM_HEADDIM = 64
SSM_INNER = SSM_HEADS * SSM_HEADDIM
SSM_GROUPS = 2
SSM_STATE = 64
SSM_CONV = 4
SSM_CHUNK = 128
SSM_XBC = SSM_INNER + 2 * SSM_GROUPS * SSM_STATE

LRU_WIDTH = 256
LRU_BLOCKS = 4
LRU_BLOCK = LRU_WIDTH // LRU_BLOCKS
LRU_CONV = 4
LRU_C = 8.0

MEM_HEADS = 4
MEM_HEAD_DIM = D_MODEL // MEM_HEADS
D_FF = 4 * D_MODEL

MLA_OUT = MLA_HEADS * MLA_V
MIX_WIDTH = MLA_OUT + SSM_INNER + LRU_WIDTH
IN_SPLITS = (
    MLA_Q_RANK,
    MLA_KV_RANK,
    MLA_ROPE,
    SSM_INNER,
    SSM_INNER,
    SSM_GROUPS * SSM_STATE,
    SSM_GROUPS * SSM_STATE,
    SSM_HEADS,
    LRU_WIDTH,
    LRU_WIDTH,
)
D_IN_PROJ = sum(IN_SPLITS)

kernel_name = "hymba_mla_ssd_rglru_trunk"


def rms_norm(x, g):
    xf = x.astype(jnp.float32)
    y = xf * lax.rsqrt(jnp.mean(xf * xf, axis=-1, keepdims=True) + EPS)
    return (y * g.astype(jnp.float32)).astype(x.dtype)


def split_cols(u, sizes):
    offsets = []
    acc = 0
    for s in sizes[:-1]:
        acc += s
        offsets.append(acc)
    return jnp.split(u, offsets, axis=-1)


def causal_depthwise_conv(x, w, b):
    k_width = w.shape[0]
    s = x.shape[1]
    xp = jnp.pad(x, ((0, 0), (k_width - 1, 0), (0, 0)))
    y = b
    for k in range(k_width):
        y = y + xp[:, k:k + s, :] * w[k]
    return y


def rope_tables(positions):
    half = MLA_ROPE // 2
    inv_freq = ROPE_THETA ** (-jnp.arange(half, dtype=jnp.float32) * 2.0 / MLA_ROPE)
    ang = positions.astype(jnp.float32)[..., None] * inv_freq
    return jnp.cos(ang)[:, :, None, :], jnp.sin(ang)[:, :, None, :]


def apply_rope(x, cos, sin):
    xf = x.astype(jnp.float32)
    half = MLA_ROPE // 2
    x1, x2 = xf[..., :half], xf[..., half:]
    out = jnp.concatenate([x1 * cos - x2 * sin, x2 * cos + x1 * sin], axis=-1)
    return out.astype(x.dtype)


def causal_block_attention(q, k, v, scale):
    b, s, h, dk = q.shape
    dv = v.shape[-1]
    nb = s // Q_BLOCK
    qb = q.reshape(b, nb, Q_BLOCK, h, dk).transpose(1, 0, 2, 3, 4)
    kpos = jnp.arange(s)

    def one_block(args):
        qi, i = args
        sc = jnp.einsum('bqhd,bkhd->bhqk', qi, k).astype(jnp.float32) * scale
        qpos = i * Q_BLOCK + jnp.arange(Q_BLOCK)
        mask = kpos[None, :] <= qpos[:, None]
        sc = jnp.where(mask[None, None], sc, -jnp.inf)
        p = jax.nn.softmax(sc, axis=-1).astype(v.dtype)
        return jnp.einsum('bhqk,bkhd->bqhd', p, v)

    out = lax.map(one_block, (qb, jnp.arange(nb)))
    return out.transpose(1, 0, 2, 3, 4).reshape(b, s, h * dv)


def mla_group(c_q, c_kv, k_rope_raw, cos, sin, q_norm_g, kv_norm_g, w_uq, w_ukv):
    b, s, _ = c_q.shape
    q = (rms_norm(c_q, q_norm_g) @ w_uq).reshape(b, s, MLA_HEADS, MLA_NOPE + MLA_ROPE)
    q_nope, q_rope = q[..., :MLA_NOPE], q[..., MLA_NOPE:]
    q = jnp.concatenate([q_nope, apply_rope(q_rope, cos, sin)], axis=-1)
    kv = (rms_norm(c_kv, kv_norm_g) @ w_ukv).reshape(b, s, MLA_HEADS, MLA_NOPE + MLA_V)
    k_nope, v = kv[..., :MLA_NOPE], kv[..., MLA_NOPE:]
    k_rope = apply_rope(k_rope_raw[:, :, None, :], cos, sin)
    k = jnp.concatenate([k_nope, jnp.broadcast_to(k_rope, (b, s, MLA_HEADS, MLA_ROPE))], axis=-1)
    scale = 1.0 / math.sqrt(MLA_NOPE + MLA_ROPE)
    return causal_block_attention(q, k, v, scale)


def segsum(a):
    t = a.shape[-1]
    a_rep = jnp.broadcast_to(a[..., :, None], a.shape + (t,))
    strict = jnp.tril(jnp.ones((t, t), dtype=bool), -1)
    cs = jnp.cumsum(jnp.where(strict, a_rep, 0.0), axis=-2)
    incl = jnp.tril(jnp.ones((t, t), dtype=bool), 0)
    return jnp.where(incl, cs, -jnp.inf)


def ssd_chunked(x, dt, a_neg, bm, cm):
    b, s, h, p = x.shape
    n = bm.shape[-1]
    nc = s // SSM_CHUNK
    xdt = (x.astype(jnp.float32) * dt[..., None]).reshape(b, nc, SSM_CHUNK, h, p)
    bc = bm.astype(jnp.float32).reshape(b, nc, SSM_CHUNK, h, n)
    cc = cm.astype(jnp.float32).reshape(b, nc, SSM_CHUNK, h, n)
    a = (dt * a_neg).reshape(b, nc, SSM_CHUNK, h).transpose(0, 3, 1, 2)
    a_cum = jnp.cumsum(a, axis=-1)
    lmat = jnp.exp(segsum(a))
    scores = jnp.einsum('bclhn,bcshn->bhcls', cc, bc) * lmat
    y_diag = jnp.einsum('bhcls,bcshp->bclhp', scores, xdt)
    decay_states = jnp.exp(a_cum[..., -1:] - a_cum)
    states = jnp.einsum('bclhn,bhcl,bclhp->bchpn', bc, decay_states, xdt)
    chunk_tot = jnp.pad(a_cum[..., -1], ((0, 0), (0, 0), (1, 0)))
    decay_chunk = jnp.exp(segsum(chunk_tot))
    states = jnp.concatenate([jnp.zeros_like(states[:, :1]), states], axis=1)
    states = jnp.einsum('bhzc,bchpn->bzhpn', decay_chunk, states)[:, :-1]
    y_off = jnp.einsum('bclhn,bchpn,bhcl->bclhp', cc, states, jnp.exp(a_cum))
    return (y_diag + y_off).reshape(b, s, h, p)


def mamba2_group(z, xs, bs, cs, dt_raw, conv_w, conv_b, dt_bias, a_log, d_skip, norm_g):
    b, s, _ = xs.shape
    xbc = jax.nn.silu(causal_depthwise_conv(jnp.concatenate([xs, bs, cs], axis=-1), conv_w, conv_b))
    xs, bs, cs = split_cols(xbc, (SSM_INNER, SSM_GROUPS * SSM_STATE, SSM_GROUPS * SSM_STATE))
    dt = jax.nn.softplus(dt_raw.astype(jnp.float32) + dt_bias.astype(jnp.float32))
    a_neg = -jnp.exp(a_log.astype(jnp.float32))
    xh = xs.reshape(b, s, SSM_HEADS, SSM_HEADDIM)
    rep = SSM_HEADS // SSM_GROUPS
    bm = jnp.repeat(bs.reshape(b, s, SSM_GROUPS, SSM_STATE), rep, axis=2)
    cm = jnp.repeat(cs.reshape(b, s, SSM_GROUPS, SSM_STATE), rep, axis=2)
    y = ssd_chunked(xh, dt, a_neg, bm, cm) + d_skip.astype(jnp.float32)[:, None] * xh.astype(jnp.float32)
    y = y.reshape(b, s, SSM_INNER) * jax.nn.silu(z.astype(jnp.float32))
    yg = y.reshape(b, s, SSM_GROUPS, SSM_INNER // SSM_GROUPS)
    yg = yg * lax.rsqrt(jnp.mean(yg * yg, axis=-1, keepdims=True) + EPS)
    return (yg.reshape(b, s, SSM_INNER) * norm_g.astype(jnp.float32)).astype(xs.dtype)


def lru_combine(c1, c2):
    a1, b1 = c1
    a2, b2 = c2
    return a1 * a2, a2 * b1 + b2


def rglru_group(xr, gate, conv_w, conv_b, w_a, b_a, w_i, b_i, lam):
    b, s, _ = xr.shape
    xr = causal_depthwise_conv(xr, conv_w, conv_b)
    xb = xr.reshape(b, s, LRU_BLOCKS, LRU_BLOCK)
    r = jax.nn.sigmoid(jnp.einsum('bsnd,nde->bsne', xb, w_a) + b_a).reshape(b, s, LRU_WIDTH)
    i = jax.nn.sigmoid(jnp.einsum('bsnd,nde->bsne', xb, w_i) + b_i).reshape(b, s, LRU_WIDTH)
    log_a = -LRU_C * r.astype(jnp.float32) * jax.nn.softplus(-lam.astype(jnp.float32))
    a = jnp.exp(log_a)
    u = jnp.sqrt(-jnp.expm1(2.0 * log_a)) * (i * xr).astype(jnp.float32)
    _, h = lax.associative_scan(lru_combine, (a, u), axis=1)
    return (h * jax.nn.gelu(gate.astype(jnp.float32), approximate=True)).astype(xr.dtype)


def memory_cross_attention(h, m, w_mq, w_mk, w_mv, w_mo):
    b, s, _ = h.shape
    q = (h @ w_mq).reshape(b, s, MEM_HEADS, MEM_HEAD_DIM)
    k = (m @ w_mk).reshape(b, N_MEM, MEM_HEADS, MEM_HEAD_DIM)
    v = (m @ w_mv).reshape(b, N_MEM, MEM_HEADS, MEM_HEAD_DIM)
    sc = jnp.einsum('bshd,bmhd->bhsm', q, k).astype(jnp.float32) / math.sqrt(MEM_HEAD_DIM)
    p = jax.nn.softmax(sc, axis=-1).astype(v.dtype)
    o = jnp.einsum('bhsm,bmhd->bshd', p, v).reshape(b, s, D_MODEL)
    return o @ w_mo


def setup_inputs(seed: int = 0) -> dict:
    key = jax.random.key(seed)
    ks = iter(jax.random.split(key, 64))
    L = DEPTH

    def nrm(shape, scale):
        return jax.random.normal(next(ks), shape, jnp.float32) * scale

    def gain(shape):
        return 1.0 + nrm(shape, 0.02)

    x = nrm((BATCH, SEQ, D_MODEL), 1.0)
    mem = nrm((BATCH, N_MEM, D_MODEL), 1.0)
    offset = jax.random.randint(next(ks), (BATCH, 1), 0, 1024, dtype=jnp.int32)
    positions = offset + jnp.arange(SEQ, dtype=jnp.int32)[None, :]

    dt0 = jnp.exp(jax.random.uniform(next(ks), (L, SSM_HEADS), jnp.float32, math.log(1e-3), math.log(1e-1)))
    ssm_dt_bias = dt0 + jnp.log(-jnp.expm1(-dt0))
    ssm_a_log = jnp.log(jax.random.uniform(next(ks), (L, SSM_HEADS), jnp.float32, 1.0, 16.0))
    a_c = jax.random.uniform(next(ks), (L, LRU_WIDTH), jnp.float32, 0.9, 0.999)
    sig = a_c ** (1.0 / LRU_C)
    lru_lambda = jnp.log(sig) - jnp.log1p(-sig)

    return {
        "x": x,
        "mem": mem,
        "positions": positions,
        "mix_norm_g": gain((L, D_MODEL)),
        "w_in": nrm((L, D_MODEL, D_IN_PROJ), D_MODEL ** -0.5),
        "mla_q_norm_g": gain((L, MLA_Q_RANK)),
        "mla_kv_norm_g": gain((L, MLA_KV_RANK)),
        "mla_w_uq": nrm((L, MLA_Q_RANK, MLA_HEADS * (MLA_NOPE + MLA_ROPE)), MLA_Q_RANK ** -0.5),
        "mla_w_ukv": nrm((L, MLA_KV_RANK, MLA_HEADS * (MLA_NOPE + MLA_V)), MLA_KV_RANK ** -0.5),
        "mla_out_g": gain((L, MLA_OUT)),
        "ssm_conv_w": nrm((L, SSM_CONV, SSM_XBC), SSM_CONV ** -0.5),
        "ssm_conv_b": nrm((L, SSM_XBC), 0.01),
        "ssm_dt_bias": ssm_dt_bias,
        "ssm_a_log": ssm_a_log,
        "ssm_d": gain((L, SSM_HEADS)),
        "ssm_norm_g": gain((L, SSM_INNER)),
        "lru_conv_w": nrm((L, LRU_CONV, LRU_WIDTH), LRU_CONV ** -0.5),
        "lru_conv_b": nrm((L, LRU_WIDTH), 0.01),
        "lru_w_a": nrm((L, LRU_BLOCKS, LRU_BLOCK, LRU_BLOCK), LRU_BLOCK ** -0.5),
        "lru_b_a": nrm((L, LRU_BLOCKS, LRU_BLOCK), 0.01),
        "lru_w_i": nrm((L, LRU_BLOCKS, LRU_BLOCK, LRU_BLOCK), LRU_BLOCK ** -0.5),
        "lru_b_i": nrm((L, LRU_BLOCKS, LRU_BLOCK), 0.01),
        "lru_lambda": lru_lambda,
        "lru_out_g": gain((L, LRU_WIDTH)),
        "w_out": nrm((L, MIX_WIDTH, D_MODEL), MIX_WIDTH ** -0.5),
        "xattn_norm_g": gain((L, D_MODEL)),
        "mem_norm_g": gain((L, D_MODEL)),
        "w_mq": nrm((L, D_MODEL, D_MODEL), D_MODEL ** -0.5),
        "w_mk": nrm((L, D_MODEL, D_MODEL), D_MODEL ** -0.5),
        "w_mv": nrm((L, D_MODEL, D_MODEL), D_MODEL ** -0.5),
        "w_mo": nrm((L, D_MODEL, D_MODEL), D_MODEL ** -0.5),
        "mlp_norm_g": gain((L, D_MODEL)),
        "w_mlp1": nrm((L, D_MODEL, D_FF), D_MODEL ** -0.5),
        "w_mlp2": nrm((L, D_FF, D_MODEL), D_FF ** -0.5),
        "final_norm_g": gain((D_MODEL,)),
    }


def reference(x, mem, positions, mix_norm_g, w_in, mla_q_norm_g, mla_kv_norm_g, mla_w_uq, mla_w_ukv,
              mla_out_g, ssm_conv_w, ssm_conv_b, ssm_dt_bias, ssm_a_log, ssm_d, ssm_norm_g,
              lru_conv_w, lru_conv_b, lru_w_a, lru_b_a, lru_w_i, lru_b_i, lru_lambda, lru_out_g,
              w_out, xattn_norm_g, mem_norm_g, w_mq, w_mk, w_mv, w_mo, mlp_norm_g, w_mlp1, w_mlp2,
              final_norm_g):
    cos, sin = rope_tables(positions)
    for l in range(DEPTH):
        h = rms_norm(x, mix_norm_g[l])
        u = h @ w_in[l]
        c_q, c_kv, k_rope, z, xs, bs, cs, dt_raw, xr, gate = split_cols(u, IN_SPLITS)
        y_mla = mla_group(c_q, c_kv, k_rope, cos, sin, mla_q_norm_g[l], mla_kv_norm_g[l],
                          mla_w_uq[l], mla_w_ukv[l])
        y_ssm = mamba2_group(z, xs, bs, cs, dt_raw, ssm_conv_w[l], ssm_conv_b[l], ssm_dt_bias[l],
                             ssm_a_log[l], ssm_d[l], ssm_norm_g[l])
        y_lru = rglru_group(xr, gate, lru_conv_w[l], lru_conv_b[l], lru_w_a[l], lru_b_a[l],
                            lru_w_i[l], lru_b_i[l], lru_lambda[l])
        y_mix = jnp.concatenate([rms_norm(y_mla, mla_out_g[l]), y_ssm, rms_norm(y_lru, lru_out_g[l])], axis=-1)
        x = x + y_mix @ w_out[l]
        x = x + memory_cross_attention(rms_norm(x, xattn_norm_g[l]), rms_norm(mem, mem_norm_g[l]),
                                       w_mq[l], w_mk[l], w_mv[l], w_mo[l])
        hm = rms_norm(x, mlp_norm_g[l]) @ w_mlp1[l]
        x = x + jnp.square(jax.nn.relu(hm)) @ w_mlp2[l]
    return rms_norm(x, final_norm_g)
```

```python
import functools
import math

import numpy as np
import jax
import jax.numpy as jnp
from jax import lax
from jax.experimental import pallas as pl
from jax.experimental.pallas import tpu as pltpu

F32 = jnp.float32
BF16 = jnp.bfloat16

EPS = 1e-6
LANE = 128
CONV_TAIL = 8

MLA_HEADS = 8
MLA_NOPE = 64
MLA_ROPE = 32
MLA_V = 64
MLA_Q_RANK = 256
MLA_KV_RANK = 128
ROPE_THETA = 10000.0
MLA_KDIM = 2 * LANE

SSM_HEADS = 4
SSM_HEADDIM = 64
SSM_INNER = 256
SSM_STATE = 64
SSM_CONV = 4
SSM_XBC = 512
SSM_COLS = 256 + SSM_XBC + LANE

LRU_WIDTH = 256
LRU_BLOCKS = 4
LRU_CONV = 4
LRU_C = 8.0

MEM_HEADS = 4
N_MEM = 256

MLA_COLS = MLA_Q_RANK + MLA_KV_RANK + 2 * LANE
LRU_COLS = 2 * LRU_WIDTH

VMEM_LIMIT = 48 * 1024 * 1024


def _cparams(*sem):
    return pltpu.CompilerParams(dimension_semantics=sem, vmem_limit_bytes=VMEM_LIMIT)


def _rms(x, g):
    ms = jnp.mean(x * x, axis=-1, keepdims=True)
    return x * lax.rsqrt(ms + EPS) * g


def _sigmoid(x):
    return 1.0 / (1.0 + jnp.exp(-x))


def _softplus(x):
    return jnp.maximum(x, 0.0) + jnp.log1p(jnp.exp(-jnp.abs(x)))


def _dot(a, b):
    return jnp.dot(a, b, preferred_element_type=F32)


def _dot_nt(a, b):
    return lax.dot_general(a, b, (((1,), (1,)), ((), ())), preferred_element_type=F32)


def _full(shape):
    zeros = (0,) * len(shape)
    return pl.BlockSpec(shape, lambda *_: zeros)


def _rope_kernel(pos_ref, freq_ref, cos_ref, sin_ref):
    ang = pos_ref[...].astype(F32) * freq_ref[...]
    cos_ref[...] = jnp.cos(ang)
    sin_ref[...] = jnp.sin(ang)


def _rope_tables(positions, tm):
    t = positions.size
    half = MLA_ROPE // 2
    inv_freq = ROPE_THETA ** (-jnp.arange(half, dtype=F32) * 2.0 / MLA_ROPE)
    freq = jnp.tile(inv_freq, LANE // half).reshape(1, LANE)
    out = jax.ShapeDtypeStruct((t, LANE), F32)
    return pl.pallas_call(
        _rope_kernel,
        out_shape=(out, out),
        grid=(t // tm,),
        in_specs=[pl.BlockSpec((tm, 1), lambda i: (i, 0)), _full((1, LANE))],
        out_specs=(pl.BlockSpec((tm, LANE), lambda i: (i, 0)),) * 2,
        compiler_params=_cparams("parallel"),
        name="rope_tables",
    )(positions.reshape(t, 1), freq)


def _mem_kv_kernel(mem_ref, g_ref, wk_ref, wv_ref, k_ref, v_ref):
    mn = _rms(mem_ref[0], g_ref[0]).astype(BF16)
    k_ref[0, 0] = _dot(mn, wk_ref[0]).astype(BF16)
    v_ref[0, 0] = _dot(mn, wv_ref[0]).astype(BF16)


def _mem_kv(mem, g, wk, wv):
    depth, d = g.shape
    b = mem.shape[0]
    out = jax.ShapeDtypeStruct((depth, b, N_MEM, d), BF16)
    wspec = pl.BlockSpec((1, d, d), lambda l, i: (l, 0, 0))
    ospec = pl.BlockSpec((1, 1, N_MEM, d), lambda l, i: (l, i, 0, 0))
    return pl.pallas_call(
        _mem_kv_kernel,
        out_shape=(out, out),
        grid=(depth, b),
        in_specs=[pl.BlockSpec((1, N_MEM, d), lambda l, i: (i, 0, 0)),
                  pl.BlockSpec((1, 1, d), lambda l, i: (l, 0, 0)), wspec, wspec],
        out_specs=(ospec, ospec),
        compiler_params=_cparams("parallel", "parallel"),
        name="mem_kv",
    )(mem, g.reshape(depth, 1, d), wk, wv)


def _fold_kernel(wq_ref, wk_ref, o_ref):
    o_ref[0] = lax.dot_general(wq_ref[0, 0], wk_ref[0, 0], (((1,), (1,)), ((), ())),
                               precision=lax.Precision.HIGHEST,
                               preferred_element_type=F32).astype(BF16)


def _fold_q_absorb(wq_nope, wk):
    depth = wq_nope.shape[0]
    return pl.pallas_call(
        _fold_kernel,
        out_shape=jax.ShapeDtypeStruct((depth, MLA_Q_RANK, MLA_HEADS * MLA_KV_RANK), BF16),
        grid=(depth, MLA_HEADS),
        in_specs=[pl.BlockSpec((1, 1, MLA_Q_RANK, MLA_NOPE), lambda l, h: (l, h, 0, 0)),
                  pl.BlockSpec((1, 1, MLA_KV_RANK, MLA_NOPE), lambda l, h: (l, h, 0, 0))],
        out_specs=pl.BlockSpec((1, MLA_Q_RANK, MLA_KV_RANK), lambda l, h: (l, 0, h)),
        compiler_params=_cparams("parallel", "parallel"),
        name="fold_q_absorb",
    )(wq_nope, wk)


def _in_proj_kernel(x_ref, g_ref, w_ref, mla_ref, ssm_ref, lru_ref):
    h = _rms(x_ref[...], g_ref[...]).astype(BF16)
    u = _dot(h, w_ref[...])
    mla_ref[...] = u[:, :MLA_COLS]
    ssm_ref[...] = u[:, MLA_COLS:MLA_COLS + SSM_COLS]
    lru_ref[...] = u[:, MLA_COLS + SSM_COLS:]


def _in_proj(x, g, w, tm):
    t, d = x.shape
    n = w.shape[1]
    row = lambda c: pl.BlockSpec((tm, c), lambda i: (i, 0))
    return pl.pallas_call(
        _in_proj_kernel,
        out_shape=(jax.ShapeDtypeStruct((t, MLA_COLS), F32),
                   jax.ShapeDtypeStruct((t, SSM_COLS), F32),
                   jax.ShapeDtypeStruct((t, LRU_COLS), F32)),
        grid=(t // tm,),
        in_specs=[row(d), _full((1, d)), _full((d, n))],
        out_specs=(row(MLA_COLS), row(SSM_COLS), row(LRU_COLS)),
        compiler_params=_cparams("parallel"),
        name="in_proj",
    )(x, g, w)


def _mla_prep_kernel(u_ref, cos_ref, sin_ref, gq_ref, gkv_ref, wqa_ref, wqr_ref, psel_ref,
                     q_ref, kc_ref):
    scale = 1.0 / math.sqrt(MLA_NOPE + MLA_ROPE)
    u = u_ref[...]
    c = cos_ref[...]
    s = sin_ref[...]
    cqn = _rms(u[:, :MLA_Q_RANK], gq_ref[...]).astype(BF16)
    qlat = _dot(cqn, wqa_ref[...]) * scale
    qr = _dot(cqn, wqr_ref[...])
    r1 = qr[:, :LANE]
    r2 = qr[:, LANE:]
    roped = jnp.concatenate([r1 * c - r2 * s, r2 * c + r1 * s], axis=1) * scale
    qsel = _dot(roped.astype(BF16), psel_ref[...])
    for h in range(MLA_HEADS):
        sl = slice(h * LANE, (h + 1) * LANE)
        q_ref[0, h] = jnp.concatenate([qlat[:, sl], qsel[:, sl]], axis=1).astype(BF16)
    o = MLA_Q_RANK
    ckvn = _rms(u[:, o:o + MLA_KV_RANK], gkv_ref[...])
    lane = lax.broadcasted_iota(jnp.int32, (1, LANE), 1)
    sgn = jnp.where(lane % MLA_ROPE < MLA_ROPE // 2, -1.0, 1.0)
    o += MLA_KV_RANK
    kr = u[:, o:o + LANE] * c + u[:, o + LANE:o + 2 * LANE] * (s * sgn)
    kc_ref[0] = jnp.concatenate([ckvn, kr], axis=1).astype(BF16)


def _mla_prep(u_mla, cos_t, sin_t, gq, gkv, wqa, wqr, psel, b, s, tm):
    ns = s // tm
    tok = lambda c: pl.BlockSpec((tm, c), lambda i, j: (i * ns + j, 0))
    return pl.pallas_call(
        _mla_prep_kernel,
        out_shape=(jax.ShapeDtypeStruct((b, MLA_HEADS, s, MLA_KDIM), BF16),
                   jax.ShapeDtypeStruct((b, s, MLA_KDIM), BF16)),
        grid=(b, ns),
        in_specs=[tok(MLA_COLS), tok(LANE), tok(LANE), _full(gq.shape), _full(gkv.shape),
                  _full(wqa.shape), _full(wqr.shape), _full(psel.shape)],
        out_specs=(pl.BlockSpec((1, MLA_HEADS, tm, MLA_KDIM), lambda i, j: (i, 0, j, 0)),
                   pl.BlockSpec((1, tm, MLA_KDIM), lambda i, j: (i, j, 0))),
        compiler_params=_cparams("parallel", "parallel"),
        name="mla_prep",
    )(u_mla, cos_t, sin_t, gq, gkv, wqa, wqr, psel)


def _attn_kernel(q_ref, kc_ref, wuv_ref, y_ref, m_ref, l_ref, acc_ref, *, tq):
    qi = pl.program_id(1)
    rows = MLA_HEADS * tq
    q = q_ref[0].reshape(rows, MLA_KDIM)
    m_ref[...] = jnp.full(m_ref.shape, -jnp.inf, F32)
    l_ref[...] = jnp.zeros(l_ref.shape, F32)
    acc_ref[...] = jnp.zeros(acc_ref.shape, F32)

    def step(j, diagonal):
        k = kc_ref[0, pl.ds(pl.multiple_of(j * tq, tq), tq), :]
        s = _dot_nt(q, k)
        if diagonal:
            qpos = lax.broadcasted_iota(jnp.int32, (rows, tq), 0) & (tq - 1)
            kpos = lax.broadcasted_iota(jnp.int32, (rows, tq), 1)
            s = jnp.where(kpos <= qpos, s, -jnp.inf)
        m_prev = m_ref[...]
        m_new = jnp.maximum(m_prev, jnp.max(s, axis=1, keepdims=True))
        alpha = jnp.exp(m_prev - m_new)
        p = jnp.exp(s - m_new)
        l_ref[...] = alpha * l_ref[...] + jnp.sum(p, axis=1, keepdims=True)
        acc_ref[...] = alpha * acc_ref[...] + _dot(p.astype(BF16), k[:, :MLA_KV_RANK])
        m_ref[...] = m_new

    def body(j, carry):
        step(j, False)
        return carry

    lax.fori_loop(0, qi, body, 0)
    step(qi, True)

    o = (acc_ref[...] * (1.0 / l_ref[...])).astype(BF16)
    ys = []
    for pr in range(MLA_HEADS // 2):
        pair = jnp.concatenate([o[(2 * pr) * tq:(2 * pr + 1) * tq],
                                o[(2 * pr + 1) * tq:(2 * pr + 2) * tq]], axis=1)
        ys.append(_dot(pair, wuv_ref[pr]))
    y_ref[0] = jnp.concatenate(ys, axis=1)


def _attention(q, kc, wuv, tq):
    b, h, s, kd = q.shape
    rows = h * tq
    return pl.pallas_call(
        functools.partial(_attn_kernel, tq=tq),
        out_shape=jax.ShapeDtypeStruct((b, s, h * MLA_V), F32),
        grid=(b, s // tq),
        in_specs=[pl.BlockSpec((1, h, tq, kd), lambda i, j: (i, 0, j, 0)),
                  pl.BlockSpec((1, s, kd), lambda i, j: (i, 0, 0)),
                  _full(wuv.shape)],
        out_specs=pl.BlockSpec((1, tq, h * MLA_V), lambda i, j: (i, j, 0)),
        scratch_shapes=[pltpu.VMEM((rows, 1), F32), pltpu.VMEM((rows, 1), F32),
                        pltpu.VMEM((rows, MLA_KV_RANK), F32)],
        compiler_params=_cparams("parallel", "arbitrary"),
        name="mla_attention",
    )(q, kc, wuv)


def _causal_conv(xbuf_ref, x, w_ref, b_ref, width):
    rows = x.shape[0]
    xbuf_ref[CONV_TAIL:CONV_TAIL + rows, :] = x
    y = b_ref[...]
    for k in range(width):
        y = y + w_ref[k:k + 1, :] * xbuf_ref[pl.ds(CONV_TAIL - width + 1 + k, rows), :]
    xbuf_ref[0:CONV_TAIL, :] = x[rows - CONV_TAIL:, :]
    return y


def _row_cumsum(x):
    rows = x.shape[0]
    row = lax.broadcasted_iota(jnp.int32, x.shape, 0)
    d = 1
    while d < rows:
        x = x + jnp.where(row >= d, pltpu.roll(x, d, axis=0), 0.0)
        d *= 2
    return x


def _ssd_kernel(u_ref, cw_ref, cb_ref, dtb_ref, alog_ref, dsk_ref, ng_ref, y_ref,
                xbuf_ref, st_ref, *, chunk):
    @pl.when(pl.program_id(1) == 0)
    def _():
        xbuf_ref[0:CONV_TAIL, :] = jnp.zeros((CONV_TAIL, SSM_XBC), F32)
        st_ref[...] = jnp.zeros(st_ref.shape, F32)

    u = u_ref[0]
    z = u[:, :SSM_INNER]
    xbc = _causal_conv(xbuf_ref, u[:, SSM_INNER:SSM_INNER + SSM_XBC], cw_ref, cb_ref, SSM_CONV)
    xbc = xbc * _sigmoid(xbc)
    xs = xbc[:, :SSM_INNER]
    bs = xbc[:, SSM_INNER:SSM_INNER + LANE]
    cs = xbc[:, SSM_INNER + LANE:]
    dt = _softplus(u[:, SSM_INNER + SSM_XBC:] + dtb_ref[...])
    a = dt * (-jnp.exp(alog_ref[...]))
    acum = _row_cumsum(a)
    acum_t = acum.T
    bs_t = bs.T

    lane = lax.broadcasted_iota(jnp.int32, (1, LANE), 1)
    lo = lane < SSM_HEADDIM
    sub = lax.broadcasted_iota(jnp.int32, (LANE, 1), 0)
    ri = lax.broadcasted_iota(jnp.int32, (chunk, chunk), 0)
    ci = lax.broadcasted_iota(jnp.int32, (chunk, chunk), 1)
    causal = ri >= ci

    ys = []
    for g in range(2):
        h0, h1 = 2 * g, 2 * g + 1
        gmask = (lane >= g * SSM_STATE) & (lane < (g + 1) * SSM_STATE)
        csg = jnp.where(gmask, cs, 0.0)
        gram = _dot_nt(csg.astype(BF16), bs.astype(BF16))
        sc = []
        dec = []
        for h in (h0, h1):
            col = acum[:, h:h + 1]
            rw = acum_t[h:h + 1, :]
            lmat = jnp.exp(jnp.where(causal, col - rw, -jnp.inf))
            sc.append((gram * lmat).astype(BF16))
            tot = acum[chunk - 1:chunk, h:h + 1]
            dec.append(jnp.exp(tot - rw))
        dtg = jnp.where(lo, dt[:, h0:h0 + 1], dt[:, h1:h1 + 1])
        xg = xs[:, g * LANE:(g + 1) * LANE]
        xdt = xg * dtg
        rhs = jnp.concatenate([jnp.where(lo, xdt, 0.0), jnp.where(lo, 0.0, xdt)], axis=0).astype(BF16)
        y_diag = _dot(jnp.concatenate(sc, axis=1), rhs)
        st = st_ref[g]
        eg = jnp.where(lo, jnp.exp(acum[:, h0:h0 + 1]), jnp.exp(acum[:, h1:h1 + 1]))
        y_off = _dot(csg.astype(BF16), st.astype(BF16)) * eg
        bsg_t = jnp.where((sub >= g * SSM_STATE) & (sub < (g + 1) * SSM_STATE), bs_t, 0.0)
        lhs = jnp.concatenate([bsg_t * dec[0], bsg_t * dec[1]], axis=1).astype(BF16)
        etot = jnp.where(lo, jnp.exp(acum[chunk - 1:chunk, h0:h0 + 1]),
                         jnp.exp(acum[chunk - 1:chunk, h1:h1 + 1]))
        st_ref[g] = etot * st + _dot(lhs, rhs)
        ys.append(y_diag + y_off)
    y = jnp.concatenate(ys, axis=1) + dsk_ref[...] * xs
    y = y * (z * _sigmoid(z))
    outs = []
    for g in range(2):
        yg = y[:, g * LANE:(g + 1) * LANE]
        outs.append(yg * lax.rsqrt(jnp.mean(yg * yg, axis=-1, keepdims=True) + EPS))
    y_ref[0] = jnp.concatenate(outs, axis=1) * ng_ref[...]


def _ssd(u_ssm, cw, cb, dtb, alog, dsk, ng, chunk):
    b, s, _ = u_ssm.shape
    return pl.pallas_call(
        functools.partial(_ssd_kernel, chunk=chunk),
        out_shape=jax.ShapeDtypeStruct((b, s, SSM_INNER), F32),
        grid=(b, s // chunk),
        in_specs=[pl.BlockSpec((1, chunk, SSM_COLS), lambda i, j: (i, j, 0)),
                  _full(cw.shape), _full(cb.shape), _full(dtb.shape), _full(alog.shape),
                  _full(dsk.shape), _full(ng.shape)],
        out_specs=pl.BlockSpec((1, chunk, SSM_INNER), lambda i, j: (i, j, 0)),
        scratch_shapes=[pltpu.VMEM((CONV_TAIL + chunk, SSM_XBC), F32),
                        pltpu.VMEM((2, LANE, LANE), F32)],
        compiler_params=_cparams("parallel", "arbitrary"),
        name="ssd",
    )(u_ssm, cw, cb, dtb, alog, dsk, ng)


def _lru_kernel(u_ref, cw_ref, cb_ref, wa_ref, ba_ref, wi_ref, bi_ref, lam_ref, y_ref,
                xbuf_ref, h_ref, *, rows):
    @pl.when(pl.program_id(1) == 0)
    def _():
        xbuf_ref[0:CONV_TAIL, :] = jnp.zeros((CONV_TAIL, LRU_WIDTH), F32)
        h_ref[...] = jnp.zeros(h_ref.shape, F32)

    u = u_ref[0]
    gate = u[:, LRU_WIDTH:]
    xc = _causal_conv(xbuf_ref, u[:, :LRU_WIDTH], cw_ref, cb_ref, LRU_CONV)
    xcb = xc.astype(BF16)
    r = _sigmoid(_dot(xcb, wa_ref[...]) + ba_ref[...])
    i = _sigmoid(_dot(xcb, wi_ref[...]) + bi_ref[...])
    log_a = (-LRU_C) * r * _softplus(-lam_ref[...])
    a = jnp.exp(log_a)
    b = jnp.sqrt(-jnp.tanh(log_a) * (a * a + 1.0)) * (i * xc)
    row = lax.broadcasted_iota(jnp.int32, a.shape, 0)
    d = 1
    while d < rows:
        keep = row >= d
        a_prev = jnp.where(keep, pltpu.roll(a, d, axis=0), 1.0)
        b_prev = jnp.where(keep, pltpu.roll(b, d, axis=0), 0.0)
        b = a * b_prev + b
        a = a * a_prev
        d *= 2
    h = b + a * h_ref[...]
    h_ref[...] = h[rows - 1:rows, :]
    c0 = math.sqrt(2.0 / math.pi)
    gelu = 0.5 * gate * (1.0 + jnp.tanh(c0 * (gate + 0.044715 * (gate * gate * gate))))
    y_ref[0] = h * gelu


def _lru(u_lru, cw, cb, wa, ba, wi, bi, lam, rows):
    b, s, _ = u_lru.shape
    return pl.pallas_call(
        functools.partial(_lru_kernel, rows=rows),
        out_shape=jax.ShapeDtypeStruct((b, s, LRU_WIDTH), F32),
        grid=(b, s // rows),
        in_specs=[pl.BlockSpec((1, rows, LRU_COLS), lambda i, j: (i, j, 0)),
                  _full(cw.shape), _full(cb.shape), _full(wa.shape), _full(ba.shape),
                  _full(wi.shape), _full(bi.shape), _full(lam.shape)],
        out_specs=pl.BlockSpec((1, rows, LRU_WIDTH), lambda i, j: (i, j, 0)),
        scratch_shapes=[pltpu.VMEM((CONV_TAIL + rows, LRU_WIDTH), F32),
                        pltpu.VMEM((1, LRU_WIDTH), F32)],
        compiler_params=_cparams("parallel", "arbitrary"),
        name="rglru",
    )(u_lru, cw, cb, wa, ba, wi, bi, lam)


def _mix_xattn_kernel(ymla_ref, yssm_ref, ylru_ref, x_ref, gmla_ref, glru_ref, wout_ref,
                      gx_ref, wmq_ref, mk_ref, mv_ref, wmo_ref, o_ref):
    ymix = jnp.concatenate([_rms(ymla_ref[...], gmla_ref[...]), yssm_ref[...],
                            _rms(ylru_ref[...], glru_ref[...])], axis=1).astype(BF16)
    x1 = x_ref[...] + _dot(ymix, wout_ref[...])
    d = x1.shape[1]
    hd = d // MEM_HEADS
    hq = _rms(x1, gx_ref[...]).astype(BF16)
    q = (_dot(hq, wmq_ref[...]) * (1.0 / math.sqrt(hd))).astype(BF16)
    outs = []
    for h in range(MEM_HEADS):
        sl = slice(h * hd, (h + 1) * hd)
        s = _dot_nt(q[:, sl], mk_ref[0, :, sl])
        p = jnp.exp(s - jnp.max(s, axis=1, keepdims=True))
        l = jnp.sum(p, axis=1, keepdims=True)
        outs.append((_dot(p.astype(BF16), mv_ref[0, :, sl]) * (1.0 / l)).astype(BF16))
    o_ref[...] = x1 + _dot(jnp.concatenate(outs, axis=1), wmo_ref[...])


def _mix_xattn(ymla, yssm, ylru, x, gmla, glru, wout, gx, wmq, mk, mv, wmo, s, tm):
    t, d = x.shape
    ns = s // tm
    row = lambda c: pl.BlockSpec((tm, c), lambda i: (i, 0))
    mspec = pl.BlockSpec((1, N_MEM, d), lambda i: (i // ns, 0, 0))
    return pl.pallas_call(
        _mix_xattn_kernel,
        out_shape=jax.ShapeDtypeStruct((t, d), F32),
        grid=(t // tm,),
        in_specs=[row(ymla.shape[1]), row(yssm.shape[1]), row(ylru.shape[1]), row(d),
                  _full(gmla.shape), _full(glru.shape), _full(wout.shape), _full(gx.shape),
                  _full(wmq.shape), mspec, mspec, _full(wmo.shape)],
        out_specs=row(d),
        compiler_params=_cparams("parallel"),
        name="mix_xattn",
    )(ymla, yssm, ylru, x, gmla, glru, wout, gx, wmq, mk, mv, wmo)


def _mlp_kernel(x_ref, g_ref, w1_ref, w2_ref, gf_ref, o_ref, *, ff_tile, final_norm):
    x = x_ref[...]
    h = _rms(x, g_ref[...]).astype(BF16)
    acc = x
    for j in range(w1_ref.shape[1] // ff_tile):
        sl = slice(j * ff_tile, (j + 1) * ff_tile)
        a = jnp.maximum(_dot(h, w1_ref[:, sl]), 0.0)
        acc = acc + _dot((a * a).astype(BF16), w2_ref[sl, :])
    o_ref[...] = _rms(acc, gf_ref[...]) if final_norm else acc


def _mlp(x, g, w1, w2, gf, tm, final_norm):
    t, d = x.shape
    row = pl.BlockSpec((tm, d), lambda i: (i, 0))
    once = lambda shape: pl.BlockSpec(shape, lambda i: (0, 0), pipeline_mode=pl.Buffered(1))
    return pl.pallas_call(
        functools.partial(_mlp_kernel, ff_tile=d, final_norm=final_norm),
        out_shape=jax.ShapeDtypeStruct((t, d), F32),
        grid=(t // tm,),
        in_specs=[row, _full(g.shape), once(w1.shape), once(w2.shape), _full(gf.shape)],
        out_specs=row,
        compiler_params=_cparams("parallel"),
        name="mlp",
    )(x, g, w1, w2, gf)


def _pad_cols(w, n):
    return jnp.pad(w, [(0, 0)] * (w.ndim - 1) + [(0, n - w.shape[-1])])


def _pack_w_in(w):
    half = MLA_ROPE // 2
    o = 0
    cq = w[..., o:o + MLA_Q_RANK]; o += MLA_Q_RANK
    ckv = w[..., o:o + MLA_KV_RANK]; o += MLA_KV_RANK
    k1 = w[..., o:o + half]; k2 = w[..., o + half:o + MLA_ROPE]; o += MLA_ROPE
    ssm = w[..., o:o + 256 + SSM_XBC + SSM_HEADS]; o += 256 + SSM_XBC + SSM_HEADS
    lru = w[..., o:]
    ka = _pad_cols(jnp.concatenate([k1, k2], -1), LANE)
    kb = _pad_cols(jnp.concatenate([k2, k1], -1), LANE)
    return jnp.concatenate([cq, ckv, ka, kb, _pad_cols(ssm, SSM_COLS), lru], -1).astype(BF16)


def _rope_select():
    half = MLA_ROPE // 2
    p = np.zeros((2 * LANE, MLA_HEADS * LANE), np.float32)
    for h in range(MLA_HEADS):
        for f in range(half):
            p[h * half + f, h * LANE + f] = 1.0
            p[LANE + h * half + f, h * LANE + half + f] = 1.0
    return jnp.asarray(p, BF16)


def _block_diag(w):
    depth, n, d, e = w.shape
    eye = jnp.eye(n, dtype=w.dtype)
    return jnp.einsum('lnde,nm->lndme', w, eye).reshape(depth, n * d, n * e)


def _lane_vec(v, n=LANE):
    return _pad_cols(v, n)[:, None, :]


def kernel(x, mem, positions, mix_norm_g, w_in, mla_q_norm_g, mla_kv_norm_g, mla_w_uq, mla_w_ukv, mla_out_g, ssm_conv_w, ssm_conv_b, ssm_dt_bias, ssm_a_log, ssm_d, ssm_norm_g, lru_conv_w, lru_conv_b, lru_w_a, lru_b_a, lru_w_i, lru_b_i, lru_lambda, lru_out_g, w_out, xattn_norm_g, mem_norm_g, w_mq, w_mk, w_mv, w_mo, mlp_norm_g, w_mlp1, w_mlp2, final_norm_g):
    b, s, d = x.shape
    depth = w_in.shape[0]
    t = b * s
    tm = min(512, s)
    tq = min(256, s)
    chunk = 128
    lru_rows = min(256, s)

    row3 = lambda v: v[:, None, :]
    half = MLA_ROPE // 2
    wq = mla_w_uq.reshape(depth, MLA_Q_RANK, MLA_HEADS, MLA_NOPE + MLA_ROPE)
    wq_nope = wq[..., :MLA_NOPE].transpose(0, 2, 1, 3)
    wqr = jnp.concatenate([wq[..., MLA_NOPE:MLA_NOPE + half].reshape(depth, MLA_Q_RANK, -1),
                           wq[..., MLA_NOPE + half:].reshape(depth, MLA_Q_RANK, -1)], -1).astype(BF16)
    wkv = mla_w_ukv.reshape(depth, MLA_KV_RANK, MLA_HEADS, MLA_NOPE + MLA_V)
    wk = wkv[..., :MLA_NOPE].transpose(0, 2, 1, 3)
    wv = wkv[..., MLA_NOPE:].transpose(0, 2, 1, 3)
    zv = jnp.zeros_like(wv[:, 0::2])
    wuv = jnp.concatenate([jnp.concatenate([wv[:, 0::2], zv], -1),
                           jnp.concatenate([zv, wv[:, 1::2]], -1)], 2).astype(BF16)
    psel = _rope_select()

    w_in_p = _pack_w_in(w_in)
    wa_bd = _block_diag(lru_w_a).astype(BF16)
    wi_bd = _block_diag(lru_w_i).astype(BF16)
    ba = lru_b_a.reshape(depth, 1, LRU_WIDTH)
    bi = lru_b_i.reshape(depth, 1, LRU_WIDTH)
    dsk = jnp.repeat(ssm_d, SSM_HEADDIM, axis=1)[:, None, :]
    dtb = _lane_vec(ssm_dt_bias)
    alog = _lane_vec(ssm_a_log)

    cos_t, sin_t = _rope_tables(positions, tm)
    memk, memv = _mem_kv(mem, mem_norm_g, w_mk.astype(BF16), w_mv.astype(BF16))
    wqa = _fold_q_absorb(wq_nope, wk)

    xt = x.reshape(t, d)
    for l in range(depth):
        u_mla, u_ssm, u_lru = _in_proj(xt, mix_norm_g[l][None], w_in_p[l], tm)
        q, kc = _mla_prep(u_mla, cos_t, sin_t, mla_q_norm_g[l][None], mla_kv_norm_g[l][None],
                          wqa[l], wqr[l], psel, b, s, tm)
        y_mla = _attention(q, kc, wuv[l], tq).reshape(t, -1)
        y_ssm = _ssd(u_ssm.reshape(b, s, -1), ssm_conv_w[l], ssm_conv_b[l][None], dtb[l], alog[l],
                     dsk[l], ssm_norm_g[l][None], chunk).reshape(t, -1)
        y_lru = _lru(u_lru.reshape(b, s, -1), lru_conv_w[l], lru_conv_b[l][None], wa_bd[l], ba[l],
                     wi_bd[l], bi[l], lru_lambda[l][None], lru_rows).reshape(t, -1)
        xt = _mix_xattn(y_mla, y_ssm, y_lru, xt, mla_out_g[l][None], lru_out_g[l][None],
                        w_out[l].astype(BF16), xattn_norm_g[l][None], w_mq[l].astype(BF16),
                        memk[l], memv[l], w_mo[l].astype(BF16), s, tm)
        xt = _mlp(xt, mlp_norm_g[l][None], w_mlp1[l].astype(BF16), w_mlp2[l].astype(BF16),
                  final_norm_g[None], tm, final_norm=(l == depth - 1))
    return xt.reshape(b, s, d)
```

```python
import functools
import math

import numpy as np
import jax
import jax.numpy as jnp
from jax import lax
from jax.experimental import pallas as pl
from jax.experimental.pallas import tpu as pltpu

F32 = jnp.float32
BF16 = jnp.bfloat16

EPS = 1e-6
LANE = 128
CONV_TAIL = 8

MLA_HEADS = 8
MLA_NOPE = 64
MLA_ROPE = 32
MLA_V = 64
MLA_Q_RANK = 256
MLA_KV_RANK = 128
ROPE_THETA = 10000.0
MLA_KDIM = 2 * LANE

SSM_HEADS = 4
SSM_HEADDIM = 64
SSM_INNER = 256
SSM_STATE = 64
SSM_CONV = 4
SSM_XBC = 512
SSM_COLS = 256 + SSM_XBC + LANE

LRU_WIDTH = 256
LRU_BLOCKS = 4
LRU_CONV = 4
LRU_C = 8.0

MEM_HEADS = 4
N_MEM = 256

MLA_COLS = MLA_Q_RANK + MLA_KV_RANK + 2 * LANE
LRU_COLS = 2 * LRU_WIDTH

VMEM_LIMIT = 48 * 1024 * 1024


def _cparams(*sem):
    return pltpu.CompilerParams(dimension_semantics=sem, vmem_limit_bytes=VMEM_LIMIT)


def _rms(x, g):
    ms = jnp.mean(x * x, axis=-1, keepdims=True)
    return x * lax.rsqrt(ms + EPS) * g


def _sigmoid(x):
    return 1.0 / (1.0 + jnp.exp(-x))


def _softplus(x):
    return jnp.maximum(x, 0.0) + jnp.log1p(jnp.exp(-jnp.abs(x)))


def _dot(a, b):
    return jnp.dot(a, b, preferred_element_type=F32)


def _dot_nt(a, b):
    return lax.dot_general(a, b, (((1,), (1,)), ((), ())), preferred_element_type=F32)


def _full(shape):
    zeros = (0,) * len(shape)
    return pl.BlockSpec(shape, lambda *_: zeros)


def _rope_kernel(pos_ref, freq_ref, cos_ref, sin_ref):
    ang = pos_ref[...].astype(F32) * freq_ref[...]
    cos_ref[...] = jnp.cos(ang)
    sin_ref[...] = jnp.sin(ang)


def _rope_tables(positions, tm):
    t = positions.size
    half = MLA_ROPE // 2
    inv_freq = ROPE_THETA ** (-jnp.arange(half, dtype=F32) * 2.0 / MLA_ROPE)
    freq = jnp.tile(inv_freq, LANE // half).reshape(1, LANE)
    out = jax.ShapeDtypeStruct((t, LANE), F32)
    return pl.pallas_call(
        _rope_kernel,
        out_shape=(out, out),
        grid=(t // tm,),
        in_specs=[pl.BlockSpec((tm, 1), lambda i: (i, 0)), _full((1, LANE))],
        out_specs=(pl.BlockSpec((tm, LANE), lambda i: (i, 0)),) * 2,
        compiler_params=_cparams("parallel"),
        name="rope_tables",
    )(positions.reshape(t, 1), freq)


def _mem_kv_kernel(mem_ref, g_ref, wk_ref, wv_ref, k_ref, v_ref):
    mn = _rms(mem_ref[0], g_ref[0]).astype(BF16)
    k_ref[0, 0] = _dot(mn, wk_ref[0]).astype(BF16)
    v_ref[0, 0] = _dot(mn, wv_ref[0]).astype(BF16)


def _mem_kv(mem, g, wk, wv):
    depth, d = g.shape
    b = mem.shape[0]
    out = jax.ShapeDtypeStruct((depth, b, N_MEM, d), BF16)
    wspec = pl.BlockSpec((1, d, d), lambda l, i: (l, 0, 0))
    ospec = pl.BlockSpec((1, 1, N_MEM, d), lambda l, i: (l, i, 0, 0))
    return pl.pallas_call(
        _mem_kv_kernel,
        out_shape=(out, out),
        grid=(depth, b),
        in_specs=[pl.BlockSpec((1, N_MEM, d), lambda l, i: (i, 0, 0)),
                  pl.BlockSpec((1, 1, d), lambda l, i: (l, 0, 0)), wspec, wspec],
        out_specs=(ospec, ospec),
        compiler_params=_cparams("parallel", "parallel"),
        name="mem_kv",
    )(mem, g.reshape(depth, 1, d), wk, wv)


def _fold_kernel(wq_ref, wk_ref, o_ref):
    o_ref[0] = lax.dot_general(wq_ref[0, 0], wk_ref[0, 0], (((1,), (1,)), ((), ())),
                               precision=lax.Precision.HIGHEST,
                               preferred_element_type=F32).astype(BF16)


def _fold_q_absorb(wq_nope, wk):
    depth = wq_nope.shape[0]
    return pl.pallas_call(
        _fold_kernel,
        out_shape=jax.ShapeDtypeStruct((depth, MLA_Q_RANK, MLA_HEADS * MLA_KV_RANK), BF16),
        grid=(depth, MLA_HEADS),
        in_specs=[pl.BlockSpec((1, 1, MLA_Q_RANK, MLA_NOPE), lambda l, h: (l, h, 0, 0)),
                  pl.BlockSpec((1, 1, MLA_KV_RANK, MLA_NOPE), lambda l, h: (l, h, 0, 0))],
        out_specs=pl.BlockSpec((1, MLA_Q_RANK, MLA_KV_RANK), lambda l, h: (l, 0, h)),
        compiler_params=_cparams("parallel", "parallel"),
        name="fold_q_absorb",
    )(wq_nope, wk)


def _in_proj_kernel(x_ref, g_ref, w_ref, mla_ref, ssm_ref, lru_ref):
    h = _rms(x_ref[...], g_ref[...]).astype(BF16)
    u = _dot(h, w_ref[...])
    mla_ref[...] = u[:, :MLA_COLS]
    ssm_ref[...] = u[:, MLA_COLS:MLA_COLS + SSM_COLS]
    lru_ref[...] = u[:, MLA_COLS + SSM_COLS:]


def _in_proj(x, g, w, tm):
    t, d = x.shape
    n = w.shape[1]
    row = lambda c: pl.BlockSpec((tm, c), lambda i: (i, 0))
    return pl.pallas_call(
        _in_proj_kernel,
        out_shape=(jax.ShapeDtypeStruct((t, MLA_COLS), F32),
                   jax.ShapeDtypeStruct((t, SSM_COLS), F32),
                   jax.ShapeDtypeStruct((t, LRU_COLS), F32)),
        grid=(t // tm,),
        in_specs=[row(d), _full((1, d)), _full((d, n))],
        out_specs=(row(MLA_COLS), row(SSM_COLS), row(LRU_COLS)),
        compiler_params=_cparams("parallel"),
        name="in_proj",
    )(x, g, w)


def _mla_prep_kernel(u_ref, cos_ref, sin_ref, gq_ref, gkv_ref, wqa_ref, wqr_ref, psel_ref,
                     q_ref, kc_ref):
    scale = math.log2(math.e) / math.sqrt(MLA_NOPE + MLA_ROPE)
    u = u_ref[...]
    c = cos_ref[...]
    s = sin_ref[...]
    cqn = _rms(u[:, :MLA_Q_RANK], gq_ref[...]).astype(BF16)
    qlat = _dot(cqn, wqa_ref[...]) * scale
    qr = _dot(cqn, wqr_ref[...])
    r1 = qr[:, :LANE]
    r2 = qr[:, LANE:]
    roped = jnp.concatenate([r1 * c - r2 * s, r2 * c + r1 * s], axis=1) * scale
    qsel = _dot(roped.astype(BF16), psel_ref[...])
    for h in range(MLA_HEADS):
        sl = slice(h * LANE, (h + 1) * LANE)
        q_ref[0, h] = jnp.concatenate([qlat[:, sl], qsel[:, sl]], axis=1).astype(BF16)
    o = MLA_Q_RANK
    ckvn = _rms(u[:, o:o + MLA_KV_RANK], gkv_ref[...])
    lane = lax.broadcasted_iota(jnp.int32, (1, LANE), 1)
    sgn = jnp.where(lane % MLA_ROPE < MLA_ROPE // 2, -1.0, 1.0)
    o += MLA_KV_RANK
    kr = u[:, o:o + LANE] * c + u[:, o + LANE:o + 2 * LANE] * (s * sgn)
    kc_ref[0] = jnp.concatenate([ckvn, kr], axis=1).astype(BF16)


def _mla_prep(u_mla, cos_t, sin_t, gq, gkv, wqa, wqr, psel, b, s, tm):
    ns = s // tm
    tok = lambda c: pl.BlockSpec((tm, c), lambda i, j: (i * ns + j, 0))
    return pl.pallas_call(
        _mla_prep_kernel,
        out_shape=(jax.ShapeDtypeStruct((b, MLA_HEADS, s, MLA_KDIM), BF16),
                   jax.ShapeDtypeStruct((b, s, MLA_KDIM), BF16)),
        grid=(b, ns),
        in_specs=[tok(MLA_COLS), tok(LANE), tok(LANE), _full(gq.shape), _full(gkv.shape),
                  _full(wqa.shape), _full(wqr.shape), _full(psel.shape)],
        out_specs=(pl.BlockSpec((1, MLA_HEADS, tm, MLA_KDIM), lambda i, j: (i, 0, j, 0)),
                   pl.BlockSpec((1, tm, MLA_KDIM), lambda i, j: (i, j, 0))),
        compiler_params=_cparams("parallel", "parallel"),
        name="mla_prep",
    )(u_mla, cos_t, sin_t, gq, gkv, wqa, wqr, psel)


def _attn_kernel(q_ref, kc_ref, wuv_ref, y_ref, m_ref, acc_ref, *, tq, tk):
    qi = pl.program_id(1)
    rows = MLA_HEADS * tq
    q = q_ref[0].reshape(rows, MLA_KDIM)
    m_ref[...] = jnp.full(m_ref.shape, -jnp.inf, F32)
    acc_ref[...] = jnp.zeros(acc_ref.shape, F32)
    ones = jnp.ones((tk, LANE), BF16)

    def step(j, masked):
        k = kc_ref[0, pl.ds(pl.multiple_of(j * tk, tk), tk), :]
        s = _dot_nt(q, k)
        if masked:
            qpos = qi * tq + (lax.broadcasted_iota(jnp.int32, (rows, tk), 0) & (tq - 1))
            kpos = j * tk + lax.broadcasted_iota(jnp.int32, (rows, tk), 1)
            s = jnp.where(kpos <= qpos, s, -jnp.inf)
        m_prev = m_ref[...]
        m_new = jnp.maximum(m_prev, jnp.max(s, axis=1, keepdims=True))
        alpha = jnp.exp2(m_prev - m_new)
        p = jnp.exp2(s - jnp.tile(m_new, (1, tk // LANE)))
        v = jnp.concatenate([k[:, :MLA_KV_RANK], ones], axis=1)
        acc_ref[...] = jnp.tile(alpha, (1, 2)) * acc_ref[...] + _dot(p.astype(BF16), v)
        m_ref[...] = m_new

    def body(j, carry):
        step(j, False)
        return carry

    n_full = (qi * tq) // tk
    lax.fori_loop(0, n_full, body, 0)
    step(n_full, True)

    acc = acc_ref[...]
    o = (acc[:, :MLA_KV_RANK] * (1.0 / acc[:, MLA_KV_RANK:])).astype(BF16)
    ys = []
    for pr in range(MLA_HEADS // 2):
        pair = jnp.concatenate([o[(2 * pr) * tq:(2 * pr + 1) * tq],
                                o[(2 * pr + 1) * tq:(2 * pr + 2) * tq]], axis=1)
        ys.append(_dot(pair, wuv_ref[pr]))
    y_ref[0] = jnp.concatenate(ys, axis=1)


def _attention(q, kc, wuv, tq, tk):
    b, h, s, kd = q.shape
    rows = h * tq
    return pl.pallas_call(
        functools.partial(_attn_kernel, tq=tq, tk=tk),
        out_shape=jax.ShapeDtypeStruct((b, s, h * MLA_V), F32),
        grid=(b, s // tq),
        in_specs=[pl.BlockSpec((1, h, tq, kd), lambda i, j: (i, 0, j, 0)),
                  pl.BlockSpec((1, s, kd), lambda i, j: (i, 0, 0)),
                  _full(wuv.shape)],
        out_specs=pl.BlockSpec((1, tq, h * MLA_V), lambda i, j: (i, j, 0)),
        scratch_shapes=[pltpu.VMEM((rows, LANE), F32), pltpu.VMEM((rows, 2 * LANE), F32)],
        compiler_params=_cparams("parallel", "arbitrary"),
        name="mla_attention",
    )(q, kc, wuv)


def _causal_conv(xbuf_ref, x, w_ref, b_ref, width):
    rows = x.shape[0]
    xbuf_ref[CONV_TAIL:CONV_TAIL + rows, :] = x
    y = b_ref[...]
    for k in range(width):
        y = y + w_ref[k:k + 1, :] * xbuf_ref[pl.ds(CONV_TAIL - width + 1 + k, rows), :]
    xbuf_ref[0:CONV_TAIL, :] = x[rows - CONV_TAIL:, :]
    return y


def _row_cumsum(x):
    rows = x.shape[0]
    row = lax.broadcasted_iota(jnp.int32, x.shape, 0)
    d = 1
    while d < rows:
        x = x + jnp.where(row >= d, pltpu.roll(x, d, axis=0), 0.0)
        d *= 2
    return x


def _ssd_kernel(u_ref, cw_ref, cb_ref, dtb_ref, alog_ref, dsk_ref, ng_ref, y_ref,
                xbuf_ref, st_ref, *, chunk):
    @pl.when(pl.program_id(1) == 0)
    def _():
        xbuf_ref[0:CONV_TAIL, :] = jnp.zeros((CONV_TAIL, SSM_XBC), F32)
        st_ref[...] = jnp.zeros(st_ref.shape, F32)

    u = u_ref[0]
    z = u[:, :SSM_INNER]
    xbc = _causal_conv(xbuf_ref, u[:, SSM_INNER:SSM_INNER + SSM_XBC], cw_ref, cb_ref, SSM_CONV)
    xbc = xbc * _sigmoid(xbc)
    xs = xbc[:, :SSM_INNER]
    bs = xbc[:, SSM_INNER:SSM_INNER + LANE]
    cs = xbc[:, SSM_INNER + LANE:]
    dt = _softplus(u[:, SSM_INNER + SSM_XBC:] + dtb_ref[...])
    a = dt * (-jnp.exp(alog_ref[...]))
    acum = _row_cumsum(a)
    acum_t = acum.T
    bs_t = bs.T

    lane = lax.broadcasted_iota(jnp.int32, (1, LANE), 1)
    lo = lane < SSM_HEADDIM
    sub = lax.broadcasted_iota(jnp.int32, (LANE, 1), 0)
    ri = lax.broadcasted_iota(jnp.int32, (chunk, chunk), 0)
    ci = lax.broadcasted_iota(jnp.int32, (chunk, chunk), 1)
    causal = ri >= ci

    ys = []
    for g in range(2):
        h0, h1 = 2 * g, 2 * g + 1
        gmask = (lane >= g * SSM_STATE) & (lane < (g + 1) * SSM_STATE)
        csg = jnp.where(gmask, cs, 0.0)
        gram = _dot_nt(csg.astype(BF16), bs.astype(BF16))
        sc = []
        dec = []
        for h in (h0, h1):
            col = acum[:, h:h + 1]
            rw = acum_t[h:h + 1, :]
            lmat = jnp.exp(jnp.where(causal, col - rw, -jnp.inf))
            sc.append((gram * lmat).astype(BF16))
            tot = acum[chunk - 1:chunk, h:h + 1]
            dec.append(jnp.exp(tot - rw))
        dtg = jnp.where(lo, dt[:, h0:h0 + 1], dt[:, h1:h1 + 1])
        xg = xs[:, g * LANE:(g + 1) * LANE]
        xdt = xg * dtg
        rhs = jnp.concatenate([jnp.where(lo, xdt, 0.0), jnp.where(lo, 0.0, xdt)], axis=0).astype(BF16)
        y_diag = _dot(jnp.concatenate(sc, axis=1), rhs)
        st = st_ref[g]
        eg = jnp.where(lo, jnp.exp(acum[:, h0:h0 + 1]), jnp.exp(acum[:, h1:h1 + 1]))
        y_off = _dot(csg.astype(BF16), st.astype(BF16)) * eg
        bsg_t = jnp.where((sub >= g * SSM_STATE) & (sub < (g + 1) * SSM_STATE), bs_t, 0.0)
        lhs = jnp.concatenate([bsg_t * dec[0], bsg_t * dec[1]], axis=1).astype(BF16)
        etot = jnp.where(lo, jnp.exp(acum[chunk - 1:chunk, h0:h0 + 1]),
                         jnp.exp(acum[chunk - 1:chunk, h1:h1 + 1]))
        st_ref[g] = etot * st + _dot(lhs, rhs)
        ys.append(y_diag + y_off)
    y = jnp.concatenate(ys, axis=1) + dsk_ref[...] * xs
    y = y * (z * _sigmoid(z))
    outs = []
    for g in range(2):
        yg = y[:, g * LANE:(g + 1) * LANE]
        outs.append(yg * lax.rsqrt(jnp.mean(yg * yg, axis=-1, keepdims=True) + EPS))
    y_ref[0] = jnp.concatenate(outs, axis=1) * ng_ref[...]


def _ssd(u_ssm, cw, cb, dtb, alog, dsk, ng, chunk):
    b, s, _ = u_ssm.shape
    return pl.pallas_call(
        functools.partial(_ssd_kernel, chunk=chunk),
        out_shape=jax.ShapeDtypeStruct((b, s, SSM_INNER), F32),
        grid=(b, s // chunk),
        in_specs=[pl.BlockSpec((1, chunk, SSM_COLS), lambda i, j: (i, j, 0)),
                  _full(cw.shape), _full(cb.shape), _full(dtb.shape), _full(alog.shape),
                  _full(dsk.shape), _full(ng.shape)],
        out_specs=pl.BlockSpec((1, chunk, SSM_INNER), lambda i, j: (i, j, 0)),
        scratch_shapes=[pltpu.VMEM((CONV_TAIL + chunk, SSM_XBC), F32),
                        pltpu.VMEM((2, LANE, LANE), F32)],
        compiler_params=_cparams("parallel", "arbitrary"),
        name="ssd",
    )(u_ssm, cw, cb, dtb, alog, dsk, ng)


def _lru_kernel(u_ref, cw_ref, cb_ref, wa_ref, ba_ref, wi_ref, bi_ref, lam_ref, y_ref,
                xbuf_ref, h_ref, *, rows):
    @pl.when(pl.program_id(1) == 0)
    def _():
        xbuf_ref[0:CONV_TAIL, :] = jnp.zeros((CONV_TAIL, LRU_WIDTH), F32)
        h_ref[...] = jnp.zeros(h_ref.shape, F32)

    u = u_ref[0]
    gate = u[:, LRU_WIDTH:]
    xc = _causal_conv(xbuf_ref, u[:, :LRU_WIDTH], cw_ref, cb_ref, LRU_CONV)
    xcb = xc.astype(BF16)
    r = _sigmoid(_dot(xcb, wa_ref[...]) + ba_ref[...])
    i = _sigmoid(_dot(xcb, wi_ref[...]) + bi_ref[...])
    log_a = (-LRU_C) * r * _softplus(-lam_ref[...])
    a = jnp.exp(log_a)
    b = jnp.sqrt(-jnp.tanh(log_a) * (a * a + 1.0)) * (i * xc)
    row = lax.broadcasted_iota(jnp.int32, a.shape, 0)
    d = 1
    while d < rows:
        keep = row >= d
        a_prev = jnp.where(keep, pltpu.roll(a, d, axis=0), 1.0)
        b_prev = jnp.where(keep, pltpu.roll(b, d, axis=0), 0.0)
        b = a * b_prev + b
        a = a * a_prev
        d *= 2
    h = b + a * h_ref[...]
    h_ref[...] = h[rows - 1:rows, :]
    c0 = math.sqrt(2.0 / math.pi)
    gelu = 0.5 * gate * (1.0 + jnp.tanh(c0 * (gate + 0.044715 * (gate * gate * gate))))
    y_ref[0] = h * gelu


def _lru(u_lru, cw, cb, wa, ba, wi, bi, lam, rows):
    b, s, _ = u_lru.shape
    return pl.pallas_call(
        functools.partial(_lru_kernel, rows=rows),
        out_shape=jax.ShapeDtypeStruct((b, s, LRU_WIDTH), F32),
        grid=(b, s // rows),
        in_specs=[pl.BlockSpec((1, rows, LRU_COLS), lambda i, j: (i, j, 0)),
                  _full(cw.shape), _full(cb.shape), _full(wa.shape), _full(ba.shape),
                  _full(wi.shape), _full(bi.shape), _full(lam.shape)],
        out_specs=pl.BlockSpec((1, rows, LRU_WIDTH), lambda i, j: (i, j, 0)),
        scratch_shapes=[pltpu.VMEM((CONV_TAIL + rows, LRU_WIDTH), F32),
                        pltpu.VMEM((1, LRU_WIDTH), F32)],
        compiler_params=_cparams("parallel", "arbitrary"),
        name="rglru",
    )(u_lru, cw, cb, wa, ba, wi, bi, lam)


def _mix_xattn_kernel(ymla_ref, yssm_ref, ylru_ref, x_ref, gmla_ref, glru_ref, wout_ref,
                      gx_ref, wmq_ref, mk_ref, mv_ref, wmo_ref, o_ref):
    ymix = jnp.concatenate([_rms(ymla_ref[...], gmla_ref[...]), yssm_ref[...],
                            _rms(ylru_ref[...], glru_ref[...])], axis=1).astype(BF16)
    x1 = x_ref[...] + _dot(ymix, wout_ref[...])
    d = x1.shape[1]
    hd = d // MEM_HEADS
    hq = _rms(x1, gx_ref[...]).astype(BF16)
    q = (_dot(hq, wmq_ref[...]) * (1.0 / math.sqrt(hd))).astype(BF16)
    outs = []
    for h in range(MEM_HEADS):
        sl = slice(h * hd, (h + 1) * hd)
        s = _dot_nt(q[:, sl], mk_ref[0, :, sl])
        p = jnp.exp(s - jnp.max(s, axis=1, keepdims=True))
        l = jnp.sum(p, axis=1, keepdims=True)
        outs.append((_dot(p.astype(BF16), mv_ref[0, :, sl]) * (1.0 / l)).astype(BF16))
    o_ref[...] = x1 + _dot(jnp.concatenate(outs, axis=1), wmo_ref[...])


def _mix_xattn(ymla, yssm, ylru, x, gmla, glru, wout, gx, wmq, mk, mv, wmo, s, tm):
    t, d = x.shape
    ns = s // tm
    row = lambda c: pl.BlockSpec((tm, c), lambda i: (i, 0))
    mspec = pl.BlockSpec((1, N_MEM, d), lambda i: (i // ns, 0, 0))
    return pl.pallas_call(
        _mix_xattn_kernel,
        out_shape=jax.ShapeDtypeStruct((t, d), F32),
        grid=(t // tm,),
        in_specs=[row(ymla.shape[1]), row(yssm.shape[1]), row(ylru.shape[1]), row(d),
                  _full(gmla.shape), _full(glru.shape), _full(wout.shape), _full(gx.shape),
                  _full(wmq.shape), mspec, mspec, _full(wmo.shape)],
        out_specs=row(d),
        compiler_params=_cparams("parallel"),
        name="mix_xattn",
    )(ymla, yssm, ylru, x, gmla, glru, wout, gx, wmq, mk, mv, wmo)


def _mlp_kernel(x_ref, g_ref, w1_ref, w2_ref, gf_ref, o_ref, *, ff_tile, final_norm):
    x = x_ref[...]
    h = _rms(x, g_ref[...]).astype(BF16)
    acc = x
    for j in range(w1_ref.shape[1] // ff_tile):
        sl = slice(j * ff_tile, (j + 1) * ff_tile)
        a = jnp.maximum(_dot(h, w1_ref[:, sl]), 0.0)
        acc = acc + _dot((a * a).astype(BF16), w2_ref[sl, :])
    o_ref[...] = _rms(acc, gf_ref[...]) if final_norm else acc


def _mlp(x, g, w1, w2, gf, tm, final_norm):
    t, d = x.shape
    row = pl.BlockSpec((tm, d), lambda i: (i, 0))
    once = lambda shape: pl.BlockSpec(shape, lambda i: (0, 0), pipeline_mode=pl.Buffered(1))
    return pl.pallas_call(
        functools.partial(_mlp_kernel, ff_tile=d, final_norm=final_norm),
        out_shape=jax.ShapeDtypeStruct((t, d), F32),
        grid=(t // tm,),
        in_specs=[row, _full(g.shape), once(w1.shape), once(w2.shape), _full(gf.shape)],
        out_specs=row,
        compiler_params=_cparams("parallel"),
        name="mlp",
    )(x, g, w1, w2, gf)


def _pad_cols(w, n):
    return jnp.pad(w, [(0, 0)] * (w.ndim - 1) + [(0, n - w.shape[-1])])


def _pack_w_in(w):
    half = MLA_ROPE // 2
    o = 0
    cq = w[..., o:o + MLA_Q_RANK]; o += MLA_Q_RANK
    ckv = w[..., o:o + MLA_KV_RANK]; o += MLA_KV_RANK
    k1 = w[..., o:o + half]; k2 = w[..., o + half:o + MLA_ROPE]; o += MLA_ROPE
    ssm = w[..., o:o + 256 + SSM_XBC + SSM_HEADS]; o += 256 + SSM_XBC + SSM_HEADS
    lru = w[..., o:]
    ka = _pad_cols(jnp.concatenate([k1, k2], -1), LANE)
    kb = _pad_cols(jnp.concatenate([k2, k1], -1), LANE)
    return jnp.concatenate([cq, ckv, ka, kb, _pad_cols(ssm, SSM_COLS), lru], -1).astype(BF16)


def _rope_select():
    half = MLA_ROPE // 2
    p = np.zeros((2 * LANE, MLA_HEADS * LANE), np.float32)
    for h in range(MLA_HEADS):
        for f in range(half):
            p[h * half + f, h * LANE + f] = 1.0
            p[LANE + h * half + f, h * LANE + half + f] = 1.0
    return jnp.asarray(p, BF16)


def _block_diag(w):
    depth, n, d, e = w.shape
    eye = jnp.eye(n, dtype=w.dtype)
    return jnp.einsum('lnde,nm->lndme', w, eye).reshape(depth, n * d, n * e)


def _lane_vec(v, n=LANE):
    return _pad_cols(v, n)[:, None, :]


def kernel(x, mem, positions, mix_norm_g, w_in, mla_q_norm_g, mla_kv_norm_g, mla_w_uq, mla_w_ukv, mla_out_g, ssm_conv_w, ssm_conv_b, ssm_dt_bias, ssm_a_log, ssm_d, ssm_norm_g, lru_conv_w, lru_conv_b, lru_w_a, lru_b_a, lru_w_i, lru_b_i, lru_lambda, lru_out_g, w_out, xattn_norm_g, mem_norm_g, w_mq, w_mk, w_mv, w_mo, mlp_norm_g, w_mlp1, w_mlp2, final_norm_g):
    b, s, d = x.shape
    depth = w_in.shape[0]
    t = b * s
    tm = min(512, s)
    tq = min(256, s)
    tk = min(512, s)
    chunk = 128
    lru_rows = min(256, s)

    half = MLA_ROPE // 2
    wq = mla_w_uq.reshape(depth, MLA_Q_RANK, MLA_HEADS, MLA_NOPE + MLA_ROPE)
    wq_nope = wq[..., :MLA_NOPE].transpose(0, 2, 1, 3)
    wqr = jnp.concatenate([wq[..., MLA_NOPE:MLA_NOPE + half].reshape(depth, MLA_Q_RANK, -1),
                           wq[..., MLA_NOPE + half:].reshape(depth, MLA_Q_RANK, -1)], -1).astype(BF16)
    wkv = mla_w_ukv.reshape(depth, MLA_KV_RANK, MLA_HEADS, MLA_NOPE + MLA_V)
    wk = wkv[..., :MLA_NOPE].transpose(0, 2, 1, 3)
    wv = wkv[..., MLA_NOPE:].transpose(0, 2, 1, 3)
    zv = jnp.zeros_like(wv[:, 0::2])
    wuv = jnp.concatenate([jnp.concatenate([wv[:, 0::2], zv], -1),
                           jnp.concatenate([zv, wv[:, 1::2]], -1)], 2).astype(BF16)
    psel = _rope_select()

    w_in_p = _pack_w_in(w_in)
    wa_bd = _block_diag(lru_w_a).astype(BF16)
    wi_bd = _block_diag(lru_w_i).astype(BF16)
    ba = lru_b_a.reshape(depth, 1, LRU_WIDTH)
    bi = lru_b_i.reshape(depth, 1, LRU_WIDTH)
    dsk = jnp.repeat(ssm_d, SSM_HEADDIM, axis=1)[:, None, :]
    dtb = _lane_vec(ssm_dt_bias)
    alog = _lane_vec(ssm_a_log)

    cos_t, sin_t = _rope_tables(positions, tm)
    memk, memv = _mem_kv(mem, mem_norm_g, w_mk.astype(BF16), w_mv.astype(BF16))
    wqa = _fold_q_absorb(wq_nope, wk)

    xt = x.reshape(t, d)
    for l in range(depth):
        u_mla, u_ssm, u_lru = _in_proj(xt, mix_norm_g[l][None], w_in_p[l], tm)
        q, kc = _mla_prep(u_mla, cos_t, sin_t, mla_q_norm_g[l][None], mla_kv_norm_g[l][None],
                          wqa[l], wqr[l], psel, b, s, tm)
        y_mla = _attention(q, kc, wuv[l], tq, tk).reshape(t, -1)
        y_ssm = _ssd(u_ssm.reshape(b, s, -1), ssm_conv_w[l], ssm_conv_b[l][None], dtb[l], alog[l],
                     dsk[l], ssm_norm_g[l][None], chunk).reshape(t, -1)
        y_lru = _lru(u_lru.reshape(b, s, -1), lru_conv_w[l], lru_conv_b[l][None], wa_bd[l], ba[l],
                     wi_bd[l], bi[l], lru_lambda[l][None], lru_rows).reshape(t, -1)
        xt = _mix_xattn(y_mla, y_ssm, y_lru, xt, mla_out_g[l][None], lru_out_g[l][None],
                        w_out[l].astype(BF16), xattn_norm_g[l][None], w_mq[l].astype(BF16),
                        memk[l], memv[l], w_mo[l].astype(BF16), s, tm)
        xt = _mlp(xt, mlp_norm_g[l][None], w_mlp1[l].astype(BF16), w_mlp2[l].astype(BF16),
                  final_norm_g[None], tm, final_norm=(l == depth - 1))
    return xt.reshape(b, s, d)
```

```python
import functools
import math

import numpy as np
import jax
import jax.numpy as jnp
from jax import lax
from jax.experimental import pallas as pl
from jax.experimental.pallas import tpu as pltpu

F32 = jnp.float32
BF16 = jnp.bfloat16

EPS = 1e-6
LANE = 128
CONV_TAIL = 8

MLA_HEADS = 8
MLA_NOPE = 64
MLA_ROPE = 32
MLA_V = 64
MLA_Q_RANK = 256
MLA_KV_RANK = 128
ROPE_THETA = 10000.0
MLA_KDIM = 2 * LANE

SSM_HEADS = 4
SSM_HEADDIM = 64
SSM_INNER = 256
SSM_STATE = 64
SSM_CONV = 4
SSM_XBC = 512
SSM_COLS = 256 + SSM_XBC + LANE

LRU_WIDTH = 256
LRU_BLOCKS = 4
LRU_CONV = 4
LRU_C = 8.0

MEM_HEADS = 4
N_MEM = 256

MLA_COLS = MLA_Q_RANK + MLA_KV_RANK + 2 * LANE
LRU_COLS = 2 * LRU_WIDTH

VMEM_LIMIT = 48 * 1024 * 1024


def _cparams(*sem):
    return pltpu.CompilerParams(dimension_semantics=sem, vmem_limit_bytes=VMEM_LIMIT)


def _rms(x, g):
    ms = jnp.mean(x * x, axis=-1, keepdims=True)
    return x * lax.rsqrt(ms + EPS) * g


def _sigmoid(x):
    return 1.0 / (1.0 + jnp.exp(-x))


def _softplus(x):
    return jnp.maximum(x, 0.0) + jnp.log1p(jnp.exp(-jnp.abs(x)))


def _dot(a, b):
    return jnp.dot(a, b, preferred_element_type=F32)


def _dot_nt(a, b):
    return lax.dot_general(a, b, (((1,), (1,)), ((), ())), preferred_element_type=F32)


def _full(shape):
    zeros = (0,) * len(shape)
    return pl.BlockSpec(shape, lambda *_: zeros)


def _layer(arr, l, **kw):
    idx = (l,) + (0,) * (arr.ndim - 1)
    return pl.BlockSpec((None,) + arr.shape[1:], lambda *_: idx, **kw)


def _rope_kernel(pos_ref, freq_ref, cos_ref, sin_ref):
    ang = pos_ref[...].astype(F32) * freq_ref[...]
    cos_ref[...] = jnp.cos(ang)
    sin_ref[...] = jnp.sin(ang)


def _rope_tables(positions, tm):
    t = positions.size
    half = MLA_ROPE // 2
    inv_freq = ROPE_THETA ** (-jnp.arange(half, dtype=F32) * 2.0 / MLA_ROPE)
    freq = jnp.tile(inv_freq, LANE // half).reshape(1, LANE)
    out = jax.ShapeDtypeStruct((t, LANE), F32)
    return pl.pallas_call(
        _rope_kernel,
        out_shape=(out, out),
        grid=(t // tm,),
        in_specs=[pl.BlockSpec((tm, 1), lambda i: (i, 0)), _full((1, LANE))],
        out_specs=(pl.BlockSpec((tm, LANE), lambda i: (i, 0)),) * 2,
        compiler_params=_cparams("parallel"),
        name="rope_tables",
    )(positions.reshape(t, 1), freq)


def _mem_kv_kernel(mem_ref, g_ref, wk_ref, wv_ref, k_ref, v_ref):
    mn = _rms(mem_ref[0], g_ref[0]).astype(BF16)
    k_ref[0, 0] = _dot(mn, wk_ref[0]).astype(BF16)
    v_ref[0, 0] = _dot(mn, wv_ref[0]).astype(BF16)


def _mem_kv(mem, g, wk, wv):
    depth, d = g.shape
    b = mem.shape[0]
    out = jax.ShapeDtypeStruct((depth, b, N_MEM, d), BF16)
    wspec = pl.BlockSpec((1, d, d), lambda l, i: (l, 0, 0))
    ospec = pl.BlockSpec((1, 1, N_MEM, d), lambda l, i: (l, i, 0, 0))
    return pl.pallas_call(
        _mem_kv_kernel,
        out_shape=(out, out),
        grid=(depth, b),
        in_specs=[pl.BlockSpec((1, N_MEM, d), lambda l, i: (i, 0, 0)),
                  pl.BlockSpec((1, 1, d), lambda l, i: (l, 0, 0)), wspec, wspec],
        out_specs=(ospec, ospec),
        compiler_params=_cparams("parallel", "parallel"),
        name="mem_kv",
    )(mem, g.reshape(depth, 1, d), wk, wv)


def _fold_kernel(wq_ref, wk_ref, o_ref):
    o_ref[0] = lax.dot_general(wq_ref[0, 0], wk_ref[0, 0], (((1,), (1,)), ((), ())),
                               precision=lax.Precision.HIGHEST,
                               preferred_element_type=F32).astype(BF16)


def _fold_q_absorb(wq_nope, wk):
    depth = wq_nope.shape[0]
    return pl.pallas_call(
        _fold_kernel,
        out_shape=jax.ShapeDtypeStruct((depth, MLA_Q_RANK, MLA_HEADS * MLA_KV_RANK), BF16),
        grid=(depth, MLA_HEADS),
        in_specs=[pl.BlockSpec((1, 1, MLA_Q_RANK, MLA_NOPE), lambda l, h: (l, h, 0, 0)),
                  pl.BlockSpec((1, 1, MLA_KV_RANK, MLA_NOPE), lambda l, h: (l, h, 0, 0))],
        out_specs=pl.BlockSpec((1, MLA_Q_RANK, MLA_KV_RANK), lambda l, h: (l, 0, h)),
        compiler_params=_cparams("parallel", "parallel"),
        name="fold_q_absorb",
    )(wq_nope, wk)


def _in_proj_kernel(x_ref, g_ref, w_ref, mla_ref, ssm_ref, lru_ref):
    h = _rms(x_ref[...], g_ref[...]).astype(BF16)
    u = _dot(h, w_ref[...])
    mla_ref[...] = u[:, :MLA_COLS]
    ssm_ref[...] = u[:, MLA_COLS:MLA_COLS + SSM_COLS]
    lru_ref[...] = u[:, MLA_COLS + SSM_COLS:]


def _in_proj(x, g, w, l, tm):
    t, d = x.shape
    row = lambda c: pl.BlockSpec((tm, c), lambda i: (i, 0))
    return pl.pallas_call(
        _in_proj_kernel,
        out_shape=(jax.ShapeDtypeStruct((t, MLA_COLS), F32),
                   jax.ShapeDtypeStruct((t, SSM_COLS), F32),
                   jax.ShapeDtypeStruct((t, LRU_COLS), F32)),
        grid=(t // tm,),
        in_specs=[row(d), _layer(g, l), _layer(w, l)],
        out_specs=(row(MLA_COLS), row(SSM_COLS), row(LRU_COLS)),
        compiler_params=_cparams("parallel"),
        name="in_proj",
    )(x, g, w)


def _mla_prep_kernel(u_ref, cos_ref, sin_ref, gq_ref, gkv_ref, wqa_ref, wqr_ref, psel_ref,
                     q_ref, kc_ref):
    scale = math.log2(math.e) / math.sqrt(MLA_NOPE + MLA_ROPE)
    u = u_ref[...]
    c = cos_ref[...]
    s = sin_ref[...]
    cqn = _rms(u[:, :MLA_Q_RANK], gq_ref[...]).astype(BF16)
    qlat = _dot(cqn, wqa_ref[...]) * scale
    qr = _dot(cqn, wqr_ref[...])
    r1 = qr[:, :LANE]
    r2 = qr[:, LANE:]
    roped = jnp.concatenate([r1 * c - r2 * s, r2 * c + r1 * s], axis=1) * scale
    qsel = _dot(roped.astype(BF16), psel_ref[...])
    for h in range(MLA_HEADS):
        sl = slice(h * LANE, (h + 1) * LANE)
        q_ref[0, h] = jnp.concatenate([qlat[:, sl], qsel[:, sl]], axis=1).astype(BF16)
    o = MLA_Q_RANK
    ckvn = _rms(u[:, o:o + MLA_KV_RANK], gkv_ref[...])
    lane = lax.broadcasted_iota(jnp.int32, (1, LANE), 1)
    sgn = jnp.where(lane % MLA_ROPE < MLA_ROPE // 2, -1.0, 1.0)
    o += MLA_KV_RANK
    kr = u[:, o:o + LANE] * c + u[:, o + LANE:o + 2 * LANE] * (s * sgn)
    kc_ref[0] = jnp.concatenate([ckvn, kr], axis=1).astype(BF16)


def _mla_prep(u_mla, cos_t, sin_t, gq, gkv, wqa, wqr, psel, l, b, s, tm):
    ns = s // tm
    tok = lambda c: pl.BlockSpec((tm, c), lambda i, j: (i * ns + j, 0))
    return pl.pallas_call(
        _mla_prep_kernel,
        out_shape=(jax.ShapeDtypeStruct((b, MLA_HEADS, s, MLA_KDIM), BF16),
                   jax.ShapeDtypeStruct((b, s, MLA_KDIM), BF16)),
        grid=(b, ns),
        in_specs=[tok(MLA_COLS), tok(LANE), tok(LANE), _layer(gq, l), _layer(gkv, l),
                  _layer(wqa, l), _layer(wqr, l), _full(psel.shape)],
        out_specs=(pl.BlockSpec((1, MLA_HEADS, tm, MLA_KDIM), lambda i, j: (i, 0, j, 0)),
                   pl.BlockSpec((1, tm, MLA_KDIM), lambda i, j: (i, j, 0))),
        compiler_params=_cparams("parallel", "parallel"),
        name="mla_prep",
    )(u_mla, cos_t, sin_t, gq, gkv, wqa, wqr, psel)


def _attn_kernel(q_ref, kc_ref, wuv_ref, y_ref, m_ref, acc_ref, s_ref, *, tq, tk):
    qi = pl.program_id(1)
    rows = MLA_HEADS * tq
    q = q_ref[0].reshape(rows, MLA_KDIM)
    m_ref[...] = jnp.full(m_ref.shape, -jnp.inf, F32)
    acc_ref[...] = jnp.zeros(acc_ref.shape, F32)
    ones = jnp.ones((tk, LANE), BF16)

    def ktile(j):
        return kc_ref[0, pl.ds(pl.multiple_of(j * tk, tk), tk), :]

    def scores(j):
        return _dot_nt(q, ktile(j))

    def absorb(slot, j, masked):
        if masked:
            qpos = qi * tq + (lax.broadcasted_iota(jnp.int32, (rows, tk), 0) & (tq - 1))
            kpos = j * tk + lax.broadcasted_iota(jnp.int32, (rows, tk), 1)
            load = lambda: jnp.where(kpos <= qpos, s_ref[slot], -jnp.inf)
        else:
            load = lambda: s_ref[slot]
        m_prev = m_ref[...]
        m_new = jnp.maximum(m_prev, jnp.max(load(), axis=1, keepdims=True))
        alpha = jnp.exp2(m_prev - m_new)
        p = jnp.exp2(load() - jnp.tile(m_new, (1, tk // LANE)))
        v = jnp.concatenate([ktile(j)[:, :MLA_KV_RANK], ones], axis=1)
        acc_ref[...] = jnp.tile(alpha, (1, 2)) * acc_ref[...] + _dot(p.astype(BF16), v)
        m_ref[...] = m_new

    n_full = (qi * tq) // tk
    s_ref[0] = scores(0)

    def pair(i, carry):
        j = 2 * i
        s_ref[1] = scores(j + 1)
        absorb(0, j, False)
        s_ref[0] = scores(j + 2)
        absorb(1, j + 1, False)
        return carry

    lax.fori_loop(0, n_full // 2, pair, 0)
    odd = n_full % 2

    @pl.when(odd == 1)
    def _():
        s_ref[1] = scores(n_full)
        absorb(0, n_full - 1, False)

    absorb(odd, n_full, True)

    acc = acc_ref[...]
    o = (acc[:, :MLA_KV_RANK] * (1.0 / acc[:, MLA_KV_RANK:])).astype(BF16)
    ys = []
    for pr in range(MLA_HEADS // 2):
        pair = jnp.concatenate([o[(2 * pr) * tq:(2 * pr + 1) * tq],
                                o[(2 * pr + 1) * tq:(2 * pr + 2) * tq]], axis=1)
        ys.append(_dot(pair, wuv_ref[pr]))
    y_ref[0] = jnp.concatenate(ys, axis=1)


def _attention(q, kc, wuv, l, tq, tk):
    b, h, s, kd = q.shape
    rows = h * tq
    return pl.pallas_call(
        functools.partial(_attn_kernel, tq=tq, tk=tk),
        out_shape=jax.ShapeDtypeStruct((b, s, h * MLA_V), F32),
        grid=(b, s // tq),
        in_specs=[pl.BlockSpec((1, h, tq, kd), lambda i, j: (i, 0, j, 0)),
                  pl.BlockSpec((1, s, kd), lambda i, j: (i, 0, 0)),
                  _layer(wuv, l)],
        out_specs=pl.BlockSpec((1, tq, h * MLA_V), lambda i, j: (i, j, 0)),
        scratch_shapes=[pltpu.VMEM((rows, LANE), F32), pltpu.VMEM((rows, 2 * LANE), F32),
                        pltpu.VMEM((2, rows, tk), F32)],
        compiler_params=_cparams("parallel", "arbitrary"),
        name="mla_attention",
    )(q, kc, wuv)


def _causal_conv(xbuf_ref, x, w_ref, b_ref, width):
    rows = x.shape[0]
    xbuf_ref[CONV_TAIL:CONV_TAIL + rows, :] = x
    y = b_ref[...]
    for k in range(width):
        y = y + w_ref[k:k + 1, :] * xbuf_ref[pl.ds(CONV_TAIL - width + 1 + k, rows), :]
    xbuf_ref[0:CONV_TAIL, :] = x[rows - CONV_TAIL:, :]
    return y


def _row_cumsum(x):
    rows = x.shape[0]
    row = lax.broadcasted_iota(jnp.int32, x.shape, 0)
    d = 1
    while d < rows:
        x = x + jnp.where(row >= d, pltpu.roll(x, d, axis=0), 0.0)
        d *= 2
    return x


def _ssd_chunk(xs, bs, cs, dt, a, states):
    chunk = xs.shape[0]
    acum = _row_cumsum(a)
    acum_t = acum.T
    bs_t = bs.T
    lane = lax.broadcasted_iota(jnp.int32, (1, LANE), 1)
    lo = lane < SSM_HEADDIM
    sub = lax.broadcasted_iota(jnp.int32, (LANE, 1), 0)
    ri = lax.broadcasted_iota(jnp.int32, (chunk, chunk), 0)
    ci = lax.broadcasted_iota(jnp.int32, (chunk, chunk), 1)
    causal = ri >= ci
    bs_b = bs.astype(BF16)

    ys = []
    new_states = []
    for g in range(2):
        h0, h1 = 2 * g, 2 * g + 1
        gmask = (lane >= g * SSM_STATE) & (lane < (g + 1) * SSM_STATE)
        csg = jnp.where(gmask, cs, 0.0).astype(BF16)
        gram = _dot_nt(csg, bs_b)
        sc = []
        dec = []
        for h in (h0, h1):
            col = acum[:, h:h + 1]
            rw = acum_t[h:h + 1, :]
            lmat = jnp.exp(jnp.where(causal, col - rw, -jnp.inf))
            sc.append((gram * lmat).astype(BF16))
            tot = acum[chunk - 1:chunk, h:h + 1]
            dec.append(jnp.exp(tot - rw))
        dtg = jnp.where(lo, dt[:, h0:h0 + 1], dt[:, h1:h1 + 1])
        xdt = xs[:, g * LANE:(g + 1) * LANE] * dtg
        rhs = jnp.concatenate([jnp.where(lo, xdt, 0.0), jnp.where(lo, 0.0, xdt)], axis=0).astype(BF16)
        y_diag = _dot(jnp.concatenate(sc, axis=1), rhs)
        st = states[g]
        eg = jnp.where(lo, jnp.exp(acum[:, h0:h0 + 1]), jnp.exp(acum[:, h1:h1 + 1]))
        y_off = _dot(csg, st.astype(BF16)) * eg
        bsg_t = jnp.where((sub >= g * SSM_STATE) & (sub < (g + 1) * SSM_STATE), bs_t, 0.0)
        lhs = jnp.concatenate([bsg_t * dec[0], bsg_t * dec[1]], axis=1).astype(BF16)
        etot = jnp.where(lo, jnp.exp(acum[chunk - 1:chunk, h0:h0 + 1]),
                         jnp.exp(acum[chunk - 1:chunk, h1:h1 + 1]))
        new_states.append(etot * st + _dot(lhs, rhs))
        ys.append(y_diag + y_off)
    return jnp.concatenate(ys, axis=1), new_states


def _ssd_kernel(u_ref, cw_ref, cb_ref, dtb_ref, alog_ref, dsk_ref, ng_ref, y_ref,
                xbuf_ref, st_ref, *, chunk):
    @pl.when(pl.program_id(1) == 0)
    def _():
        xbuf_ref[0:CONV_TAIL, :] = jnp.zeros((CONV_TAIL, SSM_XBC), F32)
        st_ref[...] = jnp.zeros(st_ref.shape, F32)

    u = u_ref[0]
    rows = u.shape[0]
    z = u[:, :SSM_INNER]
    xbc = _causal_conv(xbuf_ref, u[:, SSM_INNER:SSM_INNER + SSM_XBC], cw_ref, cb_ref, SSM_CONV)
    xbc = xbc * _sigmoid(xbc)
    xs = xbc[:, :SSM_INNER]
    bs = xbc[:, SSM_INNER:SSM_INNER + LANE]
    cs = xbc[:, SSM_INNER + LANE:]
    dt = _softplus(u[:, SSM_INNER + SSM_XBC:] + dtb_ref[...])
    a = dt * (-jnp.exp(alog_ref[...]))

    states = [st_ref[0], st_ref[1]]
    ys = []
    for c in range(rows // chunk):
        r = slice(c * chunk, (c + 1) * chunk)
        yc, states = _ssd_chunk(xs[r], bs[r], cs[r], dt[r], a[r], states)
        ys.append(yc)
    st_ref[0] = states[0]
    st_ref[1] = states[1]

    y = jnp.concatenate(ys, axis=0) + dsk_ref[...] * xs
    y = y * (z * _sigmoid(z))
    outs = []
    for g in range(2):
        yg = y[:, g * LANE:(g + 1) * LANE]
        outs.append(yg * lax.rsqrt(jnp.mean(yg * yg, axis=-1, keepdims=True) + EPS))
    y_ref[0] = jnp.concatenate(outs, axis=1) * ng_ref[...]


def _ssd(u_ssm, cw, cb, dtb, alog, dsk, ng, l, chunk, rows):
    b, s, _ = u_ssm.shape
    return pl.pallas_call(
        functools.partial(_ssd_kernel, chunk=chunk),
        out_shape=jax.ShapeDtypeStruct((b, s, SSM_INNER), F32),
        grid=(b, s // rows),
        in_specs=[pl.BlockSpec((1, rows, SSM_COLS), lambda i, j: (i, j, 0)),
                  _layer(cw, l), _layer(cb, l), _layer(dtb, l), _layer(alog, l),
                  _layer(dsk, l), _layer(ng, l)],
        out_specs=pl.BlockSpec((1, rows, SSM_INNER), lambda i, j: (i, j, 0)),
        scratch_shapes=[pltpu.VMEM((CONV_TAIL + rows, SSM_XBC), F32),
                        pltpu.VMEM((2, LANE, LANE), F32)],
        compiler_params=_cparams("parallel", "arbitrary"),
        name="ssd",
    )(u_ssm, cw, cb, dtb, alog, dsk, ng)


def _lru_kernel(u_ref, cw_ref, cb_ref, wa_ref, ba_ref, wi_ref, bi_ref, lam_ref, y_ref,
                xbuf_ref, h_ref, *, rows):
    @pl.when(pl.program_id(1) == 0)
    def _():
        xbuf_ref[0:CONV_TAIL, :] = jnp.zeros((CONV_TAIL, LRU_WIDTH), F32)
        h_ref[...] = jnp.zeros(h_ref.shape, F32)

    u = u_ref[0]
    gate = u[:, LRU_WIDTH:]
    xc = _causal_conv(xbuf_ref, u[:, :LRU_WIDTH], cw_ref, cb_ref, LRU_CONV)
    xcb = xc.astype(BF16)
    r = _sigmoid(_dot(xcb, wa_ref[...]) + ba_ref[...])
    i = _sigmoid(_dot(xcb, wi_ref[...]) + bi_ref[...])
    log_a = (-LRU_C) * r * _softplus(-lam_ref[...])
    a = jnp.exp(log_a)
    b = jnp.sqrt(-jnp.tanh(log_a) * (a * a + 1.0)) * (i * xc)
    row = lax.broadcasted_iota(jnp.int32, a.shape, 0)
    d = 1
    while d < rows:
        keep = row >= d
        a_prev = jnp.where(keep, pltpu.roll(a, d, axis=0), 1.0)
        b_prev = jnp.where(keep, pltpu.roll(b, d, axis=0), 0.0)
        b = a * b_prev + b
        a = a * a_prev
        d *= 2
    h = b + a * h_ref[...]
    h_ref[...] = h[rows - 1:rows, :]
    c0 = math.sqrt(2.0 / math.pi)
    gelu = 0.5 * gate * (1.0 + jnp.tanh(c0 * (gate + 0.044715 * (gate * gate * gate))))
    y_ref[0] = h * gelu


def _lru(u_lru, cw, cb, wa, ba, wi, bi, lam, l, rows):
    b, s, _ = u_lru.shape
    return pl.pallas_call(
        functools.partial(_lru_kernel, rows=rows),
        out_shape=jax.ShapeDtypeStruct((b, s, LRU_WIDTH), F32),
        grid=(b, s // rows),
        in_specs=[pl.BlockSpec((1, rows, LRU_COLS), lambda i, j: (i, j, 0)),
                  _layer(cw, l), _layer(cb, l), _layer(wa, l), _layer(ba, l),
                  _layer(wi, l), _layer(bi, l), _layer(lam, l)],
        out_specs=pl.BlockSpec((1, rows, LRU_WIDTH), lambda i, j: (i, j, 0)),
        scratch_shapes=[pltpu.VMEM((CONV_TAIL + rows, LRU_WIDTH), F32),
                        pltpu.VMEM((1, LRU_WIDTH), F32)],
        compiler_params=_cparams("parallel", "arbitrary"),
        name="rglru",
    )(u_lru, cw, cb, wa, ba, wi, bi, lam)


def _mix_xattn_kernel(ymla_ref, yssm_ref, ylru_ref, x_ref, gmla_ref, glru_ref, wout_ref,
                      gx_ref, wmq_ref, mk_ref, mv_ref, wmo_ref, o_ref):
    ymix = jnp.concatenate([_rms(ymla_ref[...], gmla_ref[...]), yssm_ref[...],
                            _rms(ylru_ref[...], glru_ref[...])], axis=1).astype(BF16)
    x1 = x_ref[...] + _dot(ymix, wout_ref[...])
    d = x1.shape[1]
    hd = d // MEM_HEADS
    hq = _rms(x1, gx_ref[...]).astype(BF16)
    q = (_dot(hq, wmq_ref[...]) * (1.0 / math.sqrt(hd))).astype(BF16)
    outs = []
    for h in range(MEM_HEADS):
        sl = slice(h * hd, (h + 1) * hd)
        s = _dot_nt(q[:, sl], mk_ref[0, :, sl])
        p = jnp.exp(s - jnp.max(s, axis=1, keepdims=True))
        l = jnp.sum(p, axis=1, keepdims=True)
        outs.append((_dot(p.astype(BF16), mv_ref[0, :, sl]) * (1.0 / l)).astype(BF16))
    o_ref[...] = x1 + _dot(jnp.concatenate(outs, axis=1), wmo_ref[...])


def _mix_xattn(ymla, yssm, ylru, x, gmla, glru, wout, gx, wmq, mk, mv, wmo, l, s, tm):
    t, d = x.shape
    ns = s // tm
    row = lambda c: pl.BlockSpec((tm, c), lambda i: (i, 0))
    mspec = pl.BlockSpec((None, 1, N_MEM, d), lambda i: (l, i // ns, 0, 0))
    return pl.pallas_call(
        _mix_xattn_kernel,
        out_shape=jax.ShapeDtypeStruct((t, d), F32),
        grid=(t // tm,),
        in_specs=[row(ymla.shape[1]), row(yssm.shape[1]), row(ylru.shape[1]), row(d),
                  _layer(gmla, l), _layer(glru, l), _layer(wout, l), _layer(gx, l),
                  _layer(wmq, l), mspec, mspec, _layer(wmo, l)],
        out_specs=row(d),
        compiler_params=_cparams("parallel"),
        name="mix_xattn",
    )(ymla, yssm, ylru, x, gmla, glru, wout, gx, wmq, mk, mv, wmo)


def _mlp_kernel(x_ref, g_ref, w1_ref, w2_ref, gf_ref, o_ref, *, ff_tile, final_norm):
    x = x_ref[...]
    h = _rms(x, g_ref[...]).astype(BF16)
    acc = x
    for j in range(w1_ref.shape[1] // ff_tile):
        sl = slice(j * ff_tile, (j + 1) * ff_tile)
        a = jnp.maximum(_dot(h, w1_ref[:, sl]), 0.0)
        acc = acc + _dot((a * a).astype(BF16), w2_ref[sl, :])
    o_ref[...] = _rms(acc, gf_ref[...]) if final_norm else acc


def _mlp(x, g, w1, w2, gf, l, tm, final_norm):
    t, d = x.shape
    row = pl.BlockSpec((tm, d), lambda i: (i, 0))
    return pl.pallas_call(
        functools.partial(_mlp_kernel, ff_tile=d, final_norm=final_norm),
        out_shape=jax.ShapeDtypeStruct((t, d), F32),
        grid=(t // tm,),
        in_specs=[row, _layer(g, l), _layer(w1, l, pipeline_mode=pl.Buffered(1)),
                  _layer(w2, l, pipeline_mode=pl.Buffered(1)), _full(gf.shape)],
        out_specs=row,
        compiler_params=_cparams("parallel"),
        name="mlp",
    )(x, g, w1, w2, gf)


def _pad_cols(w, n):
    return jnp.pad(w, [(0, 0)] * (w.ndim - 1) + [(0, n - w.shape[-1])])


def _pack_w_in(w):
    half = MLA_ROPE // 2
    o = 0
    cq = w[..., o:o + MLA_Q_RANK]; o += MLA_Q_RANK
    ckv = w[..., o:o + MLA_KV_RANK]; o += MLA_KV_RANK
    k1 = w[..., o:o + half]; k2 = w[..., o + half:o + MLA_ROPE]; o += MLA_ROPE
    ssm = w[..., o:o + 256 + SSM_XBC + SSM_HEADS]; o += 256 + SSM_XBC + SSM_HEADS
    lru = w[..., o:]
    ka = _pad_cols(jnp.concatenate([k1, k2], -1), LANE)
    kb = _pad_cols(jnp.concatenate([k2, k1], -1), LANE)
    return jnp.concatenate([cq, ckv, ka, kb, _pad_cols(ssm, SSM_COLS), lru], -1).astype(BF16)


def _rope_select():
    half = MLA_ROPE // 2
    p = np.zeros((2 * LANE, MLA_HEADS * LANE), np.float32)
    for h in range(MLA_HEADS):
        for f in range(half):
            p[h * half + f, h * LANE + f] = 1.0
            p[LANE + h * half + f, h * LANE + half + f] = 1.0
    return jnp.asarray(p, BF16)


def _block_diag(w):
    depth, n, d, e = w.shape
    eye = jnp.eye(n, dtype=w.dtype)
    return jnp.einsum('lnde,nm->lndme', w, eye).reshape(depth, n * d, n * e)


def _lane_vec(v, n=LANE):
    return _pad_cols(v, n)[:, None, :]


def kernel(x, mem, positions, mix_norm_g, w_in, mla_q_norm_g, mla_kv_norm_g, mla_w_uq, mla_w_ukv, mla_out_g, ssm_conv_w, ssm_conv_b, ssm_dt_bias, ssm_a_log, ssm_d, ssm_norm_g, lru_conv_w, lru_conv_b, lru_w_a, lru_b_a, lru_w_i, lru_b_i, lru_lambda, lru_out_g, w_out, xattn_norm_g, mem_norm_g, w_mq, w_mk, w_mv, w_mo, mlp_norm_g, w_mlp1, w_mlp2, final_norm_g):
    b, s, d = x.shape
    depth = w_in.shape[0]
    t = b * s
    tm = min(512, s)
    tq = min(256, s)
    tk = min(512, s)
    chunk = 128
    ssd_rows = min(512, s)
    lru_rows = min(256, s)

    half = MLA_ROPE // 2
    wq = mla_w_uq.reshape(depth, MLA_Q_RANK, MLA_HEADS, MLA_NOPE + MLA_ROPE)
    wq_nope = wq[..., :MLA_NOPE].transpose(0, 2, 1, 3)
    wqr = jnp.concatenate([wq[..., MLA_NOPE:MLA_NOPE + half].reshape(depth, MLA_Q_RANK, -1),
                           wq[..., MLA_NOPE + half:].reshape(depth, MLA_Q_RANK, -1)], -1).astype(BF16)
    wkv = mla_w_ukv.reshape(depth, MLA_KV_RANK, MLA_HEADS, MLA_NOPE + MLA_V)
    wk = wkv[..., :MLA_NOPE].transpose(0, 2, 1, 3)
    wv = wkv[..., MLA_NOPE:].transpose(0, 2, 1, 3)
    zv = jnp.zeros_like(wv[:, 0::2])
    wuv = jnp.concatenate([jnp.concatenate([wv[:, 0::2], zv], -1),
                           jnp.concatenate([zv, wv[:, 1::2]], -1)], 2).astype(BF16)
    psel = _rope_select()

    w_in_p = _pack_w_in(w_in)
    wa_bd = _block_diag(lru_w_a).astype(BF16)
    wi_bd = _block_diag(lru_w_i).astype(BF16)
    ba = lru_b_a.reshape(depth, 1, LRU_WIDTH)
    bi = lru_b_i.reshape(depth, 1, LRU_WIDTH)
    dsk = jnp.repeat(ssm_d, SSM_HEADDIM, axis=1)[:, None, :]
    dtb = _lane_vec(ssm_dt_bias)
    alog = _lane_vec(ssm_a_log)

    cos_t, sin_t = _rope_tables(positions, tm)
    memk, memv = _mem_kv(mem, mem_norm_g, w_mk.astype(BF16), w_mv.astype(BF16))
    wqa = _fold_q_absorb(wq_nope, wk)

    vec = lambda v: v[:, None, :]
    g_mix, g_q, g_kv = vec(mix_norm_g), vec(mla_q_norm_g), vec(mla_kv_norm_g)
    g_mla, g_lru, g_x, g_mlp = vec(mla_out_g), vec(lru_out_g), vec(xattn_norm_g), vec(mlp_norm_g)
    ssm_cb, ssm_ng = vec(ssm_conv_b), vec(ssm_norm_g)
    lru_cb, lam = vec(lru_conv_b), vec(lru_lambda)
    w_out_b, w_mq_b, w_mo_b = w_out.astype(BF16), w_mq.astype(BF16), w_mo.astype(BF16)
    w1_b, w2_b = w_mlp1.astype(BF16), w_mlp2.astype(BF16)

    xt = x.reshape(t, d)
    for l in range(depth):
        u_mla, u_ssm, u_lru = _in_proj(xt, g_mix, w_in_p, l, tm)
        q, kc = _mla_prep(u_mla, cos_t, sin_t, g_q, g_kv, wqa, wqr, psel, l, b, s, tm)
        y_mla = _attention(q, kc, wuv, l, tq, tk).reshape(t, -1)
        y_ssm = _ssd(u_ssm.reshape(b, s, -1), ssm_conv_w, ssm_cb, dtb, alog, dsk, ssm_ng,
                     l, chunk, ssd_rows).reshape(t, -1)
        y_lru = _lru(u_lru.reshape(b, s, -1), lru_conv_w, lru_cb, wa_bd, ba, wi_bd, bi, lam,
                     l, lru_rows).reshape(t, -1)
        xt = _mix_xattn(y_mla, y_ssm, y_lru, xt, g_mla, g_lru, w_out_b, g_x, w_mq_b,
                        memk, memv, w_mo_b, l, s, tm)
        xt = _mlp(xt, g_mlp, w1_b, w2_b, final_norm_g[None], l, tm, final_norm=(l == depth - 1))
    return xt.reshape(b, s, d)
```

```python
import functools
import math

import numpy as np
import jax
import jax.numpy as jnp
from jax import lax
from jax.experimental import pallas as pl
from jax.experimental.pallas import tpu as pltpu

F32 = jnp.float32
BF16 = jnp.bfloat16

EPS = 1e-6
LANE = 128
CONV_TAIL = 8

MLA_HEADS = 8
MLA_NOPE = 64
MLA_ROPE = 32
MLA_V = 64
MLA_Q_RANK = 256
MLA_KV_RANK = 128
ROPE_THETA = 10000.0
MLA_KDIM = 2 * LANE

SSM_HEADS = 4
SSM_HEADDIM = 64
SSM_INNER = 256
SSM_STATE = 64
SSM_CONV = 4
SSM_XBC = 512
SSM_COLS = 256 + SSM_XBC + LANE

LRU_WIDTH = 256
LRU_BLOCKS = 4
LRU_CONV = 4
LRU_C = 8.0

MEM_HEADS = 4
N_MEM = 256

MLA_COLS = MLA_Q_RANK + MLA_KV_RANK + 2 * LANE
LRU_COLS = 2 * LRU_WIDTH

VMEM_LIMIT = 48 * 1024 * 1024


def _cparams(*sem):
    return pltpu.CompilerParams(dimension_semantics=sem, vmem_limit_bytes=VMEM_LIMIT)


def _rms(x, g):
    ms = jnp.mean(x * x, axis=-1, keepdims=True)
    return x * lax.rsqrt(ms + EPS) * g


def _sigmoid(x):
    return 1.0 / (1.0 + jnp.exp(-x))


def _softplus(x):
    return jnp.maximum(x, 0.0) + jnp.log1p(jnp.exp(-jnp.abs(x)))


def _dot(a, b):
    return jnp.dot(a, b, preferred_element_type=F32)


def _dot_nt(a, b):
    return lax.dot_general(a, b, (((1,), (1,)), ((), ())), preferred_element_type=F32)


def _full(shape):
    zeros = (0,) * len(shape)
    return pl.BlockSpec(shape, lambda *_: zeros)


def _layer(arr, l, **kw):
    idx = (l,) + (0,) * (arr.ndim - 1)
    return pl.BlockSpec((None,) + arr.shape[1:], lambda *_: idx, **kw)


def _rope_kernel(pos_ref, freq_ref, cos_ref, sin_ref):
    ang = pos_ref[...].astype(F32) * freq_ref[...]
    cos_ref[...] = jnp.cos(ang)
    sin_ref[...] = jnp.sin(ang)


def _rope_tables(positions, tm):
    t = positions.size
    half = MLA_ROPE // 2
    inv_freq = ROPE_THETA ** (-jnp.arange(half, dtype=F32) * 2.0 / MLA_ROPE)
    freq = jnp.tile(inv_freq, LANE // half).reshape(1, LANE)
    out = jax.ShapeDtypeStruct((t, LANE), F32)
    return pl.pallas_call(
        _rope_kernel,
        out_shape=(out, out),
        grid=(t // tm,),
        in_specs=[pl.BlockSpec((tm, 1), lambda i: (i, 0)), _full((1, LANE))],
        out_specs=(pl.BlockSpec((tm, LANE), lambda i: (i, 0)),) * 2,
        compiler_params=_cparams("parallel"),
        name="rope_tables",
    )(positions.reshape(t, 1), freq)


def _mem_kv_kernel(mem_ref, g_ref, wk_ref, wv_ref, k_ref, v_ref):
    mn = _rms(mem_ref[0], g_ref[0]).astype(BF16)
    k_ref[0, 0] = _dot(mn, wk_ref[0]).astype(BF16)
    v_ref[0, 0] = _dot(mn, wv_ref[0]).astype(BF16)


def _mem_kv(mem, g, wk, wv):
    depth, d = g.shape
    b = mem.shape[0]
    out = jax.ShapeDtypeStruct((depth, b, N_MEM, d), BF16)
    wspec = pl.BlockSpec((1, d, d), lambda l, i: (l, 0, 0))
    ospec = pl.BlockSpec((1, 1, N_MEM, d), lambda l, i: (l, i, 0, 0))
    return pl.pallas_call(
        _mem_kv_kernel,
        out_shape=(out, out),
        grid=(depth, b),
        in_specs=[pl.BlockSpec((1, N_MEM, d), lambda l, i: (i, 0, 0)),
                  pl.BlockSpec((1, 1, d), lambda l, i: (l, 0, 0)), wspec, wspec],
        out_specs=(ospec, ospec),
        compiler_params=_cparams("parallel", "parallel"),
        name="mem_kv",
    )(mem, g.reshape(depth, 1, d), wk, wv)


def _fold_kernel(wq_ref, wk_ref, o_ref):
    o_ref[0] = lax.dot_general(wq_ref[0, 0], wk_ref[0, 0], (((1,), (1,)), ((), ())),
                               precision=lax.Precision.HIGHEST,
                               preferred_element_type=F32).astype(BF16)


def _fold_q_absorb(wq_nope, wk):
    depth = wq_nope.shape[0]
    return pl.pallas_call(
        _fold_kernel,
        out_shape=jax.ShapeDtypeStruct((depth, MLA_Q_RANK, MLA_HEADS * MLA_KV_RANK), BF16),
        grid=(depth, MLA_HEADS),
        in_specs=[pl.BlockSpec((1, 1, MLA_Q_RANK, MLA_NOPE), lambda l, h: (l, h, 0, 0)),
                  pl.BlockSpec((1, 1, MLA_KV_RANK, MLA_NOPE), lambda l, h: (l, h, 0, 0))],
        out_specs=pl.BlockSpec((1, MLA_Q_RANK, MLA_KV_RANK), lambda l, h: (l, 0, h)),
        compiler_params=_cparams("parallel", "parallel"),
        name="fold_q_absorb",
    )(wq_nope, wk)


def _in_proj_kernel(x_ref, g_ref, w_ref, cos_ref, sin_ref, gq_ref, gkv_ref, wqa_ref, wqr_ref, psel_ref,
                    q_ref, kc_ref, ssm_ref, lru_ref):
    h = _rms(x_ref[...], g_ref[...]).astype(BF16)
    u = _dot(h, w_ref[...])
    ssm_ref[...] = u[:, MLA_COLS:MLA_COLS + SSM_COLS]
    lru_ref[...] = u[:, MLA_COLS + SSM_COLS:]

    scale = math.log2(math.e) / math.sqrt(MLA_NOPE + MLA_ROPE)
    c = cos_ref[...]
    s = sin_ref[...]
    cqn = _rms(u[:, :MLA_Q_RANK], gq_ref[...]).astype(BF16)
    qlat = _dot(cqn, wqa_ref[...]) * scale
    qr = _dot(cqn, wqr_ref[...])
    r1 = qr[:, :LANE]
    r2 = qr[:, LANE:]
    roped = jnp.concatenate([r1 * c - r2 * s, r2 * c + r1 * s], axis=1) * scale
    qsel = _dot(roped.astype(BF16), psel_ref[...])
    for hd in range(MLA_HEADS):
        sl = slice(hd * LANE, (hd + 1) * LANE)
        q_ref[0, hd] = jnp.concatenate([qlat[:, sl], qsel[:, sl]], axis=1).astype(BF16)
    o = MLA_Q_RANK
    ckvn = _rms(u[:, o:o + MLA_KV_RANK], gkv_ref[...])
    lane = lax.broadcasted_iota(jnp.int32, (1, LANE), 1)
    sgn = jnp.where(lane % MLA_ROPE < MLA_ROPE // 2, -1.0, 1.0)
    o += MLA_KV_RANK
    kr = u[:, o:o + LANE] * c + u[:, o + LANE:o + 2 * LANE] * (s * sgn)
    kc_ref[0] = jnp.concatenate([ckvn, kr], axis=1).astype(BF16)


def _in_proj(x, g, w, cos_t, sin_t, gq, gkv, wqa, wqr, psel, l, b, s, tm):
    t, d = x.shape
    ns = s // tm
    row = lambda c: pl.BlockSpec((tm, c), lambda i: (i, 0))
    return pl.pallas_call(
        _in_proj_kernel,
        out_shape=(jax.ShapeDtypeStruct((b, MLA_HEADS, s, MLA_KDIM), BF16),
                   jax.ShapeDtypeStruct((b, s, MLA_KDIM), BF16),
                   jax.ShapeDtypeStruct((t, SSM_COLS), F32),
                   jax.ShapeDtypeStruct((t, LRU_COLS), F32)),
        grid=(t // tm,),
        in_specs=[row(d), _layer(g, l), _layer(w, l), row(LANE), row(LANE), _layer(gq, l), _layer(gkv, l),
                  _layer(wqa, l), _layer(wqr, l), _full(psel.shape)],
        out_specs=(pl.BlockSpec((1, MLA_HEADS, tm, MLA_KDIM), lambda i: (i // ns, 0, i % ns, 0)),
                   pl.BlockSpec((1, tm, MLA_KDIM), lambda i: (i // ns, i % ns, 0)),
                   row(SSM_COLS), row(LRU_COLS)),
        compiler_params=_cparams("parallel"),
        name="in_proj",
    )(x, g, w, cos_t, sin_t, gq, gkv, wqa, wqr, psel)


def _attn_kernel(q_ref, qn_ref, kc_ref, wuv_ref, bias_ref, y_ref, m_ref, acc_ref, s_ref, *, tq, tk):
    qi = pl.program_id(1)
    rows = MLA_HEADS * tq
    q = q_ref[0].reshape(rows, MLA_KDIM)
    m_ref[...] = jnp.full(m_ref.shape, -jnp.inf, F32)
    acc_ref[...] = jnp.zeros(acc_ref.shape, F32)
    ones = jnp.ones((tk, LANE), BF16)

    def ktile(j):
        return kc_ref[0, pl.ds(pl.multiple_of(j * tk, tk), tk), :]

    def scores(j):
        return _dot_nt(q, ktile(j))

    def absorb(slot, j, masked):
        if masked:
            bias = bias_ref[qi % (tk // tq)]
            load = lambda: (s_ref[slot].reshape(MLA_HEADS, tq, tk) + bias[None]).reshape(rows, tk)
        else:
            load = lambda: s_ref[slot]
        m_prev = m_ref[...]
        m_new = jnp.maximum(m_prev, jnp.max(load(), axis=1, keepdims=True))
        alpha = jnp.exp2(m_prev - m_new)
        p = jnp.exp2(load() - jnp.tile(m_new, (1, tk // LANE)))
        v = jnp.concatenate([ktile(j)[:, :MLA_KV_RANK], ones], axis=1)
        acc_ref[...] = jnp.tile(alpha, (1, 2)) * acc_ref[...] + _dot(p.astype(BF16), v)
        m_ref[...] = m_new

    n_full = (qi * tq) // tk

    @pl.when(qi == 0)
    def _():
        s_ref[0] = scores(0)

    def pair(i, carry):
        j = 2 * i
        s_ref[1] = scores(j + 1)
        absorb(0, j, False)
        s_ref[0] = scores(j + 2)
        absorb(1, j + 1, False)
        return carry

    lax.fori_loop(0, n_full // 2, pair, 0)
    odd = n_full % 2

    def finish(slot):
        absorb(slot, n_full, True)
        s_ref[0] = _dot_nt(qn_ref[0].reshape(rows, MLA_KDIM), ktile(0))

    @pl.when(odd == 1)
    def _():
        s_ref[1] = scores(n_full)
        absorb(0, n_full - 1, False)
        finish(1)

    @pl.when(odd == 0)
    def _():
        finish(0)

    acc = acc_ref[...]
    o = (acc[:, :MLA_KV_RANK] * (1.0 / acc[:, MLA_KV_RANK:])).astype(BF16)
    ys = []
    for pr in range(MLA_HEADS // 2):
        pair = jnp.concatenate([o[(2 * pr) * tq:(2 * pr + 1) * tq],
                                o[(2 * pr + 1) * tq:(2 * pr + 2) * tq]], axis=1)
        ys.append(_dot(pair, wuv_ref[pr]))
    y_ref[0] = jnp.concatenate(ys, axis=1)


def _causal_bias(tq, tk):
    shape = (tk // tq, tq, tk)
    o, r, c = (lax.broadcasted_iota(jnp.int32, shape, d) for d in range(3))
    return jnp.where(c <= r + o * tq, 0.0, -jnp.inf).astype(F32)


def _attention(q, kc, wuv, bias, l, tq, tk):
    b, h, s, kd = q.shape
    rows = h * tq
    last = s // tq - 1
    return pl.pallas_call(
        functools.partial(_attn_kernel, tq=tq, tk=tk),
        out_shape=jax.ShapeDtypeStruct((b, s, h * MLA_V), F32),
        grid=(b, s // tq),
        in_specs=[pl.BlockSpec((1, h, tq, kd), lambda i, j: (i, 0, j, 0)),
                  pl.BlockSpec((1, h, tq, kd), lambda i, j: (i, 0, jnp.minimum(j + 1, last), 0)),
                  pl.BlockSpec((1, s, kd), lambda i, j: (i, 0, 0)),
                  _layer(wuv, l), _full(bias.shape)],
        out_specs=pl.BlockSpec((1, tq, h * MLA_V), lambda i, j: (i, j, 0)),
        scratch_shapes=[pltpu.VMEM((rows, LANE), F32), pltpu.VMEM((rows, 2 * LANE), F32),
                        pltpu.VMEM((2, rows, tk), F32)],
        compiler_params=_cparams("parallel", "arbitrary"),
        name="mla_attention",
    )(q, q, kc, wuv, bias)


def _causal_conv(xbuf_ref, x, w_ref, b_ref, width):
    rows = x.shape[0]
    xbuf_ref[CONV_TAIL:CONV_TAIL + rows, :] = x
    y = b_ref[...]
    for k in range(width):
        y = y + w_ref[k:k + 1, :] * xbuf_ref[pl.ds(CONV_TAIL - width + 1 + k, rows), :]
    xbuf_ref[0:CONV_TAIL, :] = x[rows - CONV_TAIL:, :]
    return y


def _row_cumsum(x):
    rows = x.shape[0]
    row = lax.broadcasted_iota(jnp.int32, x.shape, 0)
    d = 1
    while d < rows:
        x = x + jnp.where(row >= d, pltpu.roll(x, d, axis=0), 0.0)
        d *= 2
    return x


def _ssd_chunk(xs, bs, cs, dt, a, states):
    chunk = xs.shape[0]
    acum = _row_cumsum(a)
    acum_t = acum.T
    bs_t = bs.T
    lane = lax.broadcasted_iota(jnp.int32, (1, LANE), 1)
    lo = lane < SSM_HEADDIM
    sub = lax.broadcasted_iota(jnp.int32, (LANE, 1), 0)
    ri = lax.broadcasted_iota(jnp.int32, (chunk, chunk), 0)
    ci = lax.broadcasted_iota(jnp.int32, (chunk, chunk), 1)
    causal = ri >= ci
    bs_b = bs.astype(BF16)

    ys = []
    new_states = []
    for g in range(2):
        h0, h1 = 2 * g, 2 * g + 1
        gmask = (lane >= g * SSM_STATE) & (lane < (g + 1) * SSM_STATE)
        csg = jnp.where(gmask, cs, 0.0).astype(BF16)
        gram = _dot_nt(csg, bs_b)
        sc = []
        dec = []
        for h in (h0, h1):
            col = acum[:, h:h + 1]
            rw = acum_t[h:h + 1, :]
            lmat = jnp.exp(jnp.where(causal, col - rw, -jnp.inf))
            sc.append((gram * lmat).astype(BF16))
            tot = acum[chunk - 1:chunk, h:h + 1]
            dec.append(jnp.exp(tot - rw))
        dtg = jnp.where(lo, dt[:, h0:h0 + 1], dt[:, h1:h1 + 1])
        xdt = xs[:, g * LANE:(g + 1) * LANE] * dtg
        rhs = jnp.concatenate([jnp.where(lo, xdt, 0.0), jnp.where(lo, 0.0, xdt)], axis=0).astype(BF16)
        y_diag = _dot(jnp.concatenate(sc, axis=1), rhs)
        st = states[g]
        eg = jnp.where(lo, jnp.exp(acum[:, h0:h0 + 1]), jnp.exp(acum[:, h1:h1 + 1]))
        y_off = _dot(csg, st.astype(BF16)) * eg
        bsg_t = jnp.where((sub >= g * SSM_STATE) & (sub < (g + 1) * SSM_STATE), bs_t, 0.0)
        lhs = jnp.concatenate([bsg_t * dec[0], bsg_t * dec[1]], axis=1).astype(BF16)
        etot = jnp.where(lo, jnp.exp(acum[chunk - 1:chunk, h0:h0 + 1]),
                         jnp.exp(acum[chunk - 1:chunk, h1:h1 + 1]))
        new_states.append(etot * st + _dot(lhs, rhs))
        ys.append(y_diag + y_off)
    return jnp.concatenate(ys, axis=1), new_states


def _ssd_kernel(u_ref, cw_ref, cb_ref, dtb_ref, alog_ref, dsk_ref, ng_ref, y_ref,
                xbuf_ref, st_ref, *, chunk):
    @pl.when(pl.program_id(1) == 0)
    def _():
        xbuf_ref[0:CONV_TAIL, :] = jnp.zeros((CONV_TAIL, SSM_XBC), F32)
        st_ref[...] = jnp.zeros(st_ref.shape, F32)

    u = u_ref[0]
    rows = u.shape[0]
    z = u[:, :SSM_INNER]
    xbc = _causal_conv(xbuf_ref, u[:, SSM_INNER:SSM_INNER + SSM_XBC], cw_ref, cb_ref, SSM_CONV)
    xbc = xbc * _sigmoid(xbc)
    xs = xbc[:, :SSM_INNER]
    bs = xbc[:, SSM_INNER:SSM_INNER + LANE]
    cs = xbc[:, SSM_INNER + LANE:]
    dt = _softplus(u[:, SSM_INNER + SSM_XBC:] + dtb_ref[...])
    a = dt * (-jnp.exp(alog_ref[...]))

    states = [st_ref[0], st_ref[1]]
    ys = []
    for c in range(rows // chunk):
        r = slice(c * chunk, (c + 1) * chunk)
        yc, states = _ssd_chunk(xs[r], bs[r], cs[r], dt[r], a[r], states)
        ys.append(yc)
    st_ref[0] = states[0]
    st_ref[1] = states[1]

    y = jnp.concatenate(ys, axis=0) + dsk_ref[...] * xs
    y = y * (z * _sigmoid(z))
    outs = []
    for g in range(2):
        yg = y[:, g * LANE:(g + 1) * LANE]
        outs.append(yg * lax.rsqrt(jnp.mean(yg * yg, axis=-1, keepdims=True) + EPS))
    y_ref[0] = jnp.concatenate(outs, axis=1) * ng_ref[...]


def _ssd(u_ssm, cw, cb, dtb, alog, dsk, ng, l, chunk, rows):
    b, s, _ = u_ssm.shape
    return pl.pallas_call(
        functools.partial(_ssd_kernel, chunk=chunk),
        out_shape=jax.ShapeDtypeStruct((b, s, SSM_INNER), F32),
        grid=(b, s // rows),
        in_specs=[pl.BlockSpec((1, rows, SSM_COLS), lambda i, j: (i, j, 0)),
                  _layer(cw, l), _layer(cb, l), _layer(dtb, l), _layer(alog, l),
                  _layer(dsk, l), _layer(ng, l)],
        out_specs=pl.BlockSpec((1, rows, SSM_INNER), lambda i, j: (i, j, 0)),
        scratch_shapes=[pltpu.VMEM((CONV_TAIL + rows, SSM_XBC), F32),
                        pltpu.VMEM((2, LANE, LANE), F32)],
        compiler_params=_cparams("parallel", "arbitrary"),
        name="ssd",
    )(u_ssm, cw, cb, dtb, alog, dsk, ng)


def _lru_kernel(u_ref, cw_ref, cb_ref, wa_ref, ba_ref, wi_ref, bi_ref, lam_ref, y_ref,
                xbuf_ref, h_ref, *, rows):
    @pl.when(pl.program_id(1) == 0)
    def _():
        xbuf_ref[0:CONV_TAIL, :] = jnp.zeros((CONV_TAIL, LRU_WIDTH), F32)
        h_ref[...] = jnp.zeros(h_ref.shape, F32)

    u = u_ref[0]
    gate = u[:, LRU_WIDTH:]
    xc = _causal_conv(xbuf_ref, u[:, :LRU_WIDTH], cw_ref, cb_ref, LRU_CONV)
    xcb = xc.astype(BF16)
    r = _sigmoid(_dot(xcb, wa_ref[...]) + ba_ref[...])
    i = _sigmoid(_dot(xcb, wi_ref[...]) + bi_ref[...])
    log_a = (-LRU_C) * r * _softplus(-lam_ref[...])
    a = jnp.exp(log_a)
    b = jnp.sqrt(-jnp.tanh(log_a) * (a * a + 1.0)) * (i * xc)
    row = lax.broadcasted_iota(jnp.int32, a.shape, 0)
    d = 1
    while d < rows:
        keep = row >= d
        a_prev = jnp.where(keep, pltpu.roll(a, d, axis=0), 1.0)
        b_prev = jnp.where(keep, pltpu.roll(b, d, axis=0), 0.0)
        b = a * b_prev + b
        a = a * a_prev
        d *= 2
    h = b + a * h_ref[...]
    h_ref[...] = h[rows - 1:rows, :]
    c0 = math.sqrt(2.0 / math.pi)
    gelu = 0.5 * gate * (1.0 + jnp.tanh(c0 * (gate + 0.044715 * (gate * gate * gate))))
    y_ref[0] = h * gelu


def _lru(u_lru, cw, cb, wa, ba, wi, bi, lam, l, rows):
    b, s, _ = u_lru.shape
    return pl.pallas_call(
        functools.partial(_lru_kernel, rows=rows),
        out_shape=jax.ShapeDtypeStruct((b, s, LRU_WIDTH), F32),
        grid=(b, s // rows),
        in_specs=[pl.BlockSpec((1, rows, LRU_COLS), lambda i, j: (i, j, 0)),
                  _layer(cw, l), _layer(cb, l), _layer(wa, l), _layer(ba, l),
                  _layer(wi, l), _layer(bi, l), _layer(lam, l)],
        out_specs=pl.BlockSpec((1, rows, LRU_WIDTH), lambda i, j: (i, j, 0)),
        scratch_shapes=[pltpu.VMEM((CONV_TAIL + rows, LRU_WIDTH), F32),
                        pltpu.VMEM((1, LRU_WIDTH), F32)],
        compiler_params=_cparams("parallel", "arbitrary"),
        name="rglru",
    )(u_lru, cw, cb, wa, ba, wi, bi, lam)


def _mix_xattn_kernel(ymla_ref, yssm_ref, ylru_ref, x_ref, gmla_ref, glru_ref, wout_ref,
                      gx_ref, wmq_ref, mk_ref, mv_ref, wmo_ref, o_ref):
    ymix = jnp.concatenate([_rms(ymla_ref[...], gmla_ref[...]), yssm_ref[...],
                            _rms(ylru_ref[...], glru_ref[...])], axis=1).astype(BF16)
    x1 = x_ref[...] + _dot(ymix, wout_ref[...])
    d = x1.shape[1]
    hd = d // MEM_HEADS
    hq = _rms(x1, gx_ref[...]).astype(BF16)
    q = (_dot(hq, wmq_ref[...]) * (1.0 / math.sqrt(hd))).astype(BF16)
    outs = []
    for h in range(MEM_HEADS):
        sl = slice(h * hd, (h + 1) * hd)
        s = _dot_nt(q[:, sl], mk_ref[0, :, sl])
        p = jnp.exp(s - jnp.max(s, axis=1, keepdims=True))
        l = jnp.sum(p, axis=1, keepdims=True)
        outs.append((_dot(p.astype(BF16), mv_ref[0, :, sl]) * (1.0 / l)).astype(BF16))
    o_ref[...] = x1 + _dot(jnp.concatenate(outs, axis=1), wmo_ref[...])


def _mix_xattn(ymla, yssm, ylru, x, gmla, glru, wout, gx, wmq, mk, mv, wmo, l, s, tm):
    t, d = x.shape
    ns = s // tm
    row = lambda c: pl.BlockSpec((tm, c), lambda i: (i, 0))
    mspec = pl.BlockSpec((None, 1, N_MEM, d), lambda i: (l, i // ns, 0, 0))
    return pl.pallas_call(
        _mix_xattn_kernel,
        out_shape=jax.ShapeDtypeStruct((t, d), F32),
        grid=(t // tm,),
        in_specs=[row(ymla.shape[1]), row(yssm.shape[1]), row(ylru.shape[1]), row(d),
                  _layer(gmla, l), _layer(glru, l), _layer(wout, l), _layer(gx, l),
                  _layer(wmq, l), mspec, mspec, _layer(wmo, l)],
        out_specs=row(d),
        compiler_params=_cparams("parallel"),
        name="mix_xattn",
    )(ymla, yssm, ylru, x, gmla, glru, wout, gx, wmq, mk, mv, wmo)


def _mlp_kernel(x_ref, g_ref, w1_ref, w2_ref, gf_ref, o_ref, *, ff_tile, final_norm):
    x = x_ref[...]
    h = _rms(x, g_ref[...]).astype(BF16)
    acc = x
    for j in range(w1_ref.shape[1] // ff_tile):
        sl = slice(j * ff_tile, (j + 1) * ff_tile)
        a = jnp.maximum(_dot(h, w1_ref[:, sl]), 0.0)
        acc = acc + _dot((a * a).astype(BF16), w2_ref[sl, :])
    o_ref[...] = _rms(acc, gf_ref[...]) if final_norm else acc


def _mlp(x, g, w1, w2, gf, l, tm, final_norm):
    t, d = x.shape
    row = pl.BlockSpec((tm, d), lambda i: (i, 0))
    return pl.pallas_call(
        functools.partial(_mlp_kernel, ff_tile=d, final_norm=final_norm),
        out_shape=jax.ShapeDtypeStruct((t, d), F32),
        grid=(t // tm,),
        in_specs=[row, _layer(g, l), _layer(w1, l, pipeline_mode=pl.Buffered(1)),
                  _layer(w2, l, pipeline_mode=pl.Buffered(1)), _full(gf.shape)],
        out_specs=row,
        compiler_params=_cparams("parallel"),
        name="mlp",
    )(x, g, w1, w2, gf)


def _pad_cols(w, n):
    return jnp.pad(w, [(0, 0)] * (w.ndim - 1) + [(0, n - w.shape[-1])])


def _pack_w_in(w):
    half = MLA_ROPE // 2
    o = 0
    cq = w[..., o:o + MLA_Q_RANK]; o += MLA_Q_RANK
    ckv = w[..., o:o + MLA_KV_RANK]; o += MLA_KV_RANK
    k1 = w[..., o:o + half]; k2 = w[..., o + half:o + MLA_ROPE]; o += MLA_ROPE
    ssm = w[..., o:o + 256 + SSM_XBC + SSM_HEADS]; o += 256 + SSM_XBC + SSM_HEADS
    lru = w[..., o:]
    ka = _pad_cols(jnp.concatenate([k1, k2], -1), LANE)
    kb = _pad_cols(jnp.concatenate([k2, k1], -1), LANE)
    return jnp.concatenate([cq, ckv, ka, kb, _pad_cols(ssm, SSM_COLS), lru], -1).astype(BF16)


def _rope_select():
    half = MLA_ROPE // 2
    p = np.zeros((2 * LANE, MLA_HEADS * LANE), np.float32)
    for h in range(MLA_HEADS):
        for f in range(half):
            p[h * half + f, h * LANE + f] = 1.0
            p[LANE + h * half + f, h * LANE + half + f] = 1.0
    return jnp.asarray(p, BF16)


def _block_diag(w):
    depth, n, d, e = w.shape
    eye = jnp.eye(n, dtype=w.dtype)
    return jnp.einsum('lnde,nm->lndme', w, eye).reshape(depth, n * d, n * e)


def _lane_vec(v, n=LANE):
    return _pad_cols(v, n)[:, None, :]


def kernel(x, mem, positions, mix_norm_g, w_in, mla_q_norm_g, mla_kv_norm_g, mla_w_uq, mla_w_ukv, mla_out_g, ssm_conv_w, ssm_conv_b, ssm_dt_bias, ssm_a_log, ssm_d, ssm_norm_g, lru_conv_w, lru_conv_b, lru_w_a, lru_b_a, lru_w_i, lru_b_i, lru_lambda, lru_out_g, w_out, xattn_norm_g, mem_norm_g, w_mq, w_mk, w_mv, w_mo, mlp_norm_g, w_mlp1, w_mlp2, final_norm_g):
    b, s, d = x.shape
    depth = w_in.shape[0]
    t = b * s
    tm = min(512, s)
    tq = min(256, s)
    tk = min(512, s)
    chunk = 128
    ssd_rows = min(512, s)
    lru_rows = min(256, s)

    half = MLA_ROPE // 2
    wq = mla_w_uq.reshape(depth, MLA_Q_RANK, MLA_HEADS, MLA_NOPE + MLA_ROPE)
    wq_nope = wq[..., :MLA_NOPE].transpose(0, 2, 1, 3)
    wqr = jnp.concatenate([wq[..., MLA_NOPE:MLA_NOPE + half].reshape(depth, MLA_Q_RANK, -1),
                           wq[..., MLA_NOPE + half:].reshape(depth, MLA_Q_RANK, -1)], -1).astype(BF16)
    wkv = mla_w_ukv.reshape(depth, MLA_KV_RANK, MLA_HEADS, MLA_NOPE + MLA_V)
    wk = wkv[..., :MLA_NOPE].transpose(0, 2, 1, 3)
    wv = wkv[..., MLA_NOPE:].transpose(0, 2, 1, 3)
    zv = jnp.zeros_like(wv[:, 0::2])
    wuv = jnp.concatenate([jnp.concatenate([wv[:, 0::2], zv], -1),
                           jnp.concatenate([zv, wv[:, 1::2]], -1)], 2).astype(BF16)
    psel = _rope_select()

    w_in_p = _pack_w_in(w_in)
    wa_bd = _block_diag(lru_w_a).astype(BF16)
    wi_bd = _block_diag(lru_w_i).astype(BF16)
    ba = lru_b_a.reshape(depth, 1, LRU_WIDTH)
    bi = lru_b_i.reshape(depth, 1, LRU_WIDTH)
    dsk = jnp.repeat(ssm_d, SSM_HEADDIM, axis=1)[:, None, :]
    dtb = _lane_vec(ssm_dt_bias)
    alog = _lane_vec(ssm_a_log)

    bias = _causal_bias(tq, tk)
    cos_t, sin_t = _rope_tables(positions, tm)
    memk, memv = _mem_kv(mem, mem_norm_g, w_mk.astype(BF16), w_mv.astype(BF16))
    wqa = _fold_q_absorb(wq_nope, wk)

    vec = lambda v: v[:, None, :]
    g_mix, g_q, g_kv = vec(mix_norm_g), vec(mla_q_norm_g), vec(mla_kv_norm_g)
    g_mla, g_lru, g_x, g_mlp = vec(mla_out_g), vec(lru_out_g), vec(xattn_norm_g), vec(mlp_norm_g)
    ssm_cb, ssm_ng = vec(ssm_conv_b), vec(ssm_norm_g)
    lru_cb, lam = vec(lru_conv_b), vec(lru_lambda)
    w_out_b, w_mq_b, w_mo_b = w_out.astype(BF16), w_mq.astype(BF16), w_mo.astype(BF16)
    w1_b, w2_b = w_mlp1.astype(BF16), w_mlp2.astype(BF16)

    xt = x.reshape(t, d)
    for l in range(depth):
        q, kc, u_ssm, u_lru = _in_proj(xt, g_mix, w_in_p, cos_t, sin_t, g_q, g_kv, wqa, wqr, psel,
                                       l, b, s, tm)
        y_mla = _attention(q, kc, wuv, bias, l, tq, tk).reshape(t, -1)
        y_ssm = _ssd(u_ssm.reshape(b, s, -1), ssm_conv_w, ssm_cb, dtb, alog, dsk, ssm_ng,
                     l, chunk, ssd_rows).reshape(t, -1)
        y_lru = _lru(u_lru.reshape(b, s, -1), lru_conv_w, lru_cb, wa_bd, ba, wi_bd, bi, lam,
                     l, lru_rows).reshape(t, -1)
        xt = _mix_xattn(y_mla, y_ssm, y_lru, xt, g_mla, g_lru, w_out_b, g_x, w_mq_b,
                        memk, memv, w_mo_b, l, s, tm)
        xt = _mlp(xt, g_mlp, w1_b, w2_b, final_norm_g[None], l, tm, final_norm=(l == depth - 1))
    return xt.reshape(b, s, d)
```

```python
import functools
import math

import numpy as np
import jax
import jax.numpy as jnp
from jax import lax
from jax.experimental import pallas as pl
from jax.experimental.pallas import tpu as pltpu

F32 = jnp.float32
BF16 = jnp.bfloat16

EPS = 1e-6
LANE = 128
CONV_TAIL = 8

MLA_HEADS = 8
MLA_NOPE = 64
MLA_ROPE = 32
MLA_V = 64
MLA_Q_RANK = 256
MLA_KV_RANK = 128
ROPE_THETA = 10000.0
MLA_KDIM = 2 * LANE

SSM_HEADS = 4
SSM_HEADDIM = 64
SSM_INNER = 256
SSM_STATE = 64
SSM_CONV = 4
SSM_XBC = 512
SSM_COLS = 256 + SSM_XBC + LANE

LRU_WIDTH = 256
LRU_BLOCKS = 4
LRU_CONV = 4
LRU_C = 8.0

MEM_HEADS = 4
N_MEM = 256

MLA_COLS = MLA_Q_RANK + MLA_KV_RANK + 2 * LANE
LRU_COLS = 2 * LRU_WIDTH

VMEM_LIMIT = 48 * 1024 * 1024


def _cparams(*sem):
    return pltpu.CompilerParams(dimension_semantics=sem, vmem_limit_bytes=VMEM_LIMIT)


def _rms(x, g):
    ms = jnp.mean(x * x, axis=-1, keepdims=True)
    return x * lax.rsqrt(ms + EPS) * g


def _sigmoid(x):
    return 1.0 / (1.0 + jnp.exp(-x))


def _softplus(x):
    return jnp.maximum(x, 0.0) + jnp.log1p(jnp.exp(-jnp.abs(x)))


def _dot(a, b):
    return jnp.dot(a, b, preferred_element_type=F32)


def _dot_nt(a, b):
    return lax.dot_general(a, b, (((1,), (1,)), ((), ())), preferred_element_type=F32)


def _full(shape):
    zeros = (0,) * len(shape)
    return pl.BlockSpec(shape, lambda *_: zeros)


def _layer(arr, l, **kw):
    idx = (l,) + (0,) * (arr.ndim - 1)
    return pl.BlockSpec((None,) + arr.shape[1:], lambda *_: idx, **kw)


def _rope_kernel(pos_ref, freq_ref, cos_ref, sin_ref):
    ang = pos_ref[...].astype(F32) * freq_ref[...]
    cos_ref[...] = jnp.cos(ang)
    sin_ref[...] = jnp.sin(ang)


def _rope_tables(positions):
    t = positions.size
    half = MLA_ROPE // 2
    per_row = LANE // half
    inv_freq = ROPE_THETA ** (-jnp.arange(half, dtype=F32) * 2.0 / MLA_ROPE)
    freq = jnp.tile(inv_freq, per_row).reshape(1, LANE)
    pos = jnp.repeat(positions.reshape(t // per_row, per_row), half, axis=1)
    out = jax.ShapeDtypeStruct(pos.shape, F32)
    cos_c, sin_c = pl.pallas_call(
        _rope_kernel,
        out_shape=(out, out),
        in_specs=[_full(pos.shape), _full((1, LANE))],
        out_specs=(_full(pos.shape),) * 2,
        compiler_params=_cparams(),
        name="rope_tables",
    )(pos, freq)
    expand = lambda a: jnp.tile(a.reshape(t, half), (1, per_row))
    return expand(cos_c), expand(sin_c)


def _mem_kv_kernel(mem_ref, g_ref, wk_ref, wv_ref, k_ref, v_ref):
    mn = _rms(mem_ref[0], g_ref[0]).astype(BF16)
    k_ref[0, 0] = _dot(mn, wk_ref[0]).astype(BF16)
    v_ref[0, 0] = _dot(mn, wv_ref[0]).astype(BF16)


def _mem_kv(mem, g, wk, wv):
    depth, d = g.shape
    b = mem.shape[0]
    out = jax.ShapeDtypeStruct((depth, b, N_MEM, d), BF16)
    wspec = pl.BlockSpec((1, d, d), lambda l, i: (l, 0, 0))
    ospec = pl.BlockSpec((1, 1, N_MEM, d), lambda l, i: (l, i, 0, 0))
    return pl.pallas_call(
        _mem_kv_kernel,
        out_shape=(out, out),
        grid=(depth, b),
        in_specs=[pl.BlockSpec((1, N_MEM, d), lambda l, i: (i, 0, 0)),
                  pl.BlockSpec((1, 1, d), lambda l, i: (l, 0, 0)), wspec, wspec],
        out_specs=(ospec, ospec),
        compiler_params=_cparams("parallel", "parallel"),
        name="mem_kv",
    )(mem, g.reshape(depth, 1, d), wk, wv)


def _fold_kernel(wq_ref, wk_ref, o_ref):
    for h in range(MLA_HEADS):
        o_ref[0, :, h * MLA_KV_RANK:(h + 1) * MLA_KV_RANK] = lax.dot_general(
            wq_ref[0, h], wk_ref[0, h], (((1,), (1,)), ((), ())),
            precision=lax.Precision.HIGHEST, preferred_element_type=F32).astype(BF16)


def _fold_q_absorb(wq_nope, wk):
    depth = wq_nope.shape[0]
    return pl.pallas_call(
        _fold_kernel,
        out_shape=jax.ShapeDtypeStruct((depth, MLA_Q_RANK, MLA_HEADS * MLA_KV_RANK), BF16),
        grid=(depth,),
        in_specs=[pl.BlockSpec((1, MLA_HEADS, MLA_Q_RANK, MLA_NOPE), lambda l: (l, 0, 0, 0)),
                  pl.BlockSpec((1, MLA_HEADS, MLA_KV_RANK, MLA_NOPE), lambda l: (l, 0, 0, 0))],
        out_specs=pl.BlockSpec((1, MLA_Q_RANK, MLA_HEADS * MLA_KV_RANK), lambda l: (l, 0, 0)),
        compiler_params=_cparams("parallel"),
        name="fold_q_absorb",
    )(wq_nope, wk)


def _in_proj_kernel(x_ref, g_ref, w_ref, cos_ref, sin_ref, gq_ref, gkv_ref, wqa_ref, wqr_ref, psel_ref,
                    q_ref, kc_ref, ssm_ref, lru_ref):
    h = _rms(x_ref[...], g_ref[...]).astype(BF16)
    u = _dot(h, w_ref[...])
    ssm_ref[...] = u[:, MLA_COLS:MLA_COLS + SSM_COLS]
    lru_ref[...] = u[:, MLA_COLS + SSM_COLS:]

    scale = math.log2(math.e) / math.sqrt(MLA_NOPE + MLA_ROPE)
    c = cos_ref[...]
    s = sin_ref[...]
    cqn = _rms(u[:, :MLA_Q_RANK], gq_ref[...]).astype(BF16)
    qlat = _dot(cqn, wqa_ref[...]) * scale
    qr = _dot(cqn, wqr_ref[...])
    r1 = qr[:, :LANE]
    r2 = qr[:, LANE:]
    roped = jnp.concatenate([r1 * c - r2 * s, r2 * c + r1 * s], axis=1) * scale
    qsel = _dot(roped.astype(BF16), psel_ref[...])
    for hd in range(MLA_HEADS):
        sl = slice(hd * LANE, (hd + 1) * LANE)
        q_ref[0, hd] = jnp.concatenate([qlat[:, sl], qsel[:, sl]], axis=1).astype(BF16)
    o = MLA_Q_RANK
    ckvn = _rms(u[:, o:o + MLA_KV_RANK], gkv_ref[...])
    lane = lax.broadcasted_iota(jnp.int32, (1, LANE), 1)
    sgn = jnp.where(lane % MLA_ROPE < MLA_ROPE // 2, -1.0, 1.0)
    o += MLA_KV_RANK
    kr = u[:, o:o + LANE] * c + u[:, o + LANE:o + 2 * LANE] * (s * sgn)
    kc_ref[0] = jnp.concatenate([ckvn, kr], axis=1).astype(BF16)


def _in_proj(x, g, w, cos_t, sin_t, gq, gkv, wqa, wqr, psel, l, b, s, tm):
    t, d = x.shape
    ns = s // tm
    row = lambda c: pl.BlockSpec((tm, c), lambda i: (i, 0))
    return pl.pallas_call(
        _in_proj_kernel,
        out_shape=(jax.ShapeDtypeStruct((b, MLA_HEADS, s, MLA_KDIM), BF16),
                   jax.ShapeDtypeStruct((b, s, MLA_KDIM), BF16),
                   jax.ShapeDtypeStruct((t, SSM_COLS), F32),
                   jax.ShapeDtypeStruct((t, LRU_COLS), F32)),
        grid=(t // tm,),
        in_specs=[row(d), _layer(g, l), _layer(w, l), row(LANE), row(LANE), _layer(gq, l), _layer(gkv, l),
                  _layer(wqa, l), _layer(wqr, l), _full(psel.shape)],
        out_specs=(pl.BlockSpec((1, MLA_HEADS, tm, MLA_KDIM), lambda i: (i // ns, 0, i % ns, 0)),
                   pl.BlockSpec((1, tm, MLA_KDIM), lambda i: (i // ns, i % ns, 0)),
                   row(SSM_COLS), row(LRU_COLS)),
        compiler_params=_cparams("parallel"),
        name="in_proj",
    )(x, g, w, cos_t, sin_t, gq, gkv, wqa, wqr, psel)


def _attn_kernel(q_ref, qn_ref, kc_ref, wuv_ref, bias_ref, y_ref, m_ref, acc_ref, s_ref, *, tq, tk):
    qi = pl.program_id(1)
    rows = MLA_HEADS * tq
    q = q_ref[0].reshape(rows, MLA_KDIM)
    m_ref[...] = jnp.full(m_ref.shape, -jnp.inf, F32)
    acc_ref[...] = jnp.zeros(acc_ref.shape, F32)
    ones = jnp.ones((tk, LANE), BF16)

    def ktile(j):
        return kc_ref[0, pl.ds(pl.multiple_of(j * tk, tk), tk), :]

    def scores(j):
        return _dot_nt(q, ktile(j))

    def absorb(slot, j, masked):
        if masked:
            bias = bias_ref[qi % (tk // tq)]
            load = lambda: (s_ref[slot].reshape(MLA_HEADS, tq, tk) + bias[None]).reshape(rows, tk)
        else:
            load = lambda: s_ref[slot]
        m_prev = m_ref[...]
        m_new = jnp.maximum(m_prev, jnp.max(load(), axis=1, keepdims=True))
        alpha = jnp.exp2(m_prev - m_new)
        p = jnp.exp2(load() - jnp.tile(m_new, (1, tk // LANE)))
        v = jnp.concatenate([ktile(j)[:, :MLA_KV_RANK], ones], axis=1)
        acc_ref[...] = jnp.tile(alpha, (1, 2)) * acc_ref[...] + _dot(p.astype(BF16), v)
        m_ref[...] = m_new

    n_full = (qi * tq) // tk

    @pl.when(qi == 0)
    def _():
        s_ref[0] = scores(0)

    def pair(i, carry):
        j = 2 * i
        s_ref[1] = scores(j + 1)
        absorb(0, j, False)
        s_ref[0] = scores(j + 2)
        absorb(1, j + 1, False)
        return carry

    lax.fori_loop(0, n_full // 2, pair, 0)
    odd = n_full % 2

    def finish(slot):
        absorb(slot, n_full, True)
        s_ref[0] = _dot_nt(qn_ref[0].reshape(rows, MLA_KDIM), ktile(0))

    @pl.when(odd == 1)
    def _():
        s_ref[1] = scores(n_full)
        absorb(0, n_full - 1, False)
        finish(1)

    @pl.when(odd == 0)
    def _():
        finish(0)

    acc = acc_ref[...]
    o = (acc[:, :MLA_KV_RANK] * (1.0 / acc[:, MLA_KV_RANK:])).astype(BF16)
    ys = []
    for pr in range(MLA_HEADS // 2):
        pair = jnp.concatenate([o[(2 * pr) * tq:(2 * pr + 1) * tq],
                                o[(2 * pr + 1) * tq:(2 * pr + 2) * tq]], axis=1)
        ys.append(_dot(pair, wuv_ref[pr]))
    y_ref[0] = jnp.concatenate(ys, axis=1)


def _causal_bias(tq, tk):
    shape = (tk // tq, tq, tk)
    o, r, c = (lax.broadcasted_iota(jnp.int32, shape, d) for d in range(3))
    return jnp.where(c <= r + o * tq, 0.0, -jnp.inf).astype(F32)


def _attention(q, kc, wuv, bias, l, tq, tk):
    b, h, s, kd = q.shape
    rows = h * tq
    last = s // tq - 1
    return pl.pallas_call(
        functools.partial(_attn_kernel, tq=tq, tk=tk),
        out_shape=jax.ShapeDtypeStruct((b, s, h * MLA_V), F32),
        grid=(b, s // tq),
        in_specs=[pl.BlockSpec((1, h, tq, kd), lambda i, j: (i, 0, j, 0)),
                  pl.BlockSpec((1, h, tq, kd), lambda i, j: (i, 0, jnp.minimum(j + 1, last), 0)),
                  pl.BlockSpec((1, s, kd), lambda i, j: (i, 0, 0)),
                  _layer(wuv, l), _full(bias.shape)],
        out_specs=pl.BlockSpec((1, tq, h * MLA_V), lambda i, j: (i, j, 0)),
        scratch_shapes=[pltpu.VMEM((rows, LANE), F32), pltpu.VMEM((rows, 2 * LANE), F32),
                        pltpu.VMEM((2, rows, tk), F32)],
        compiler_params=_cparams("parallel", "arbitrary"),
        name="mla_attention",
    )(q, q, kc, wuv, bias)


def _causal_conv(xbuf_ref, x, w_ref, b_ref, width):
    rows = x.shape[0]
    xbuf_ref[CONV_TAIL:CONV_TAIL + rows, :] = x
    y = b_ref[...]
    for k in range(width):
        y = y + w_ref[k:k + 1, :] * xbuf_ref[pl.ds(CONV_TAIL - width + 1 + k, rows), :]
    xbuf_ref[0:CONV_TAIL, :] = x[rows - CONV_TAIL:, :]
    return y


def _row_cumsum(x):
    rows = x.shape[0]
    row = lax.broadcasted_iota(jnp.int32, x.shape, 0)
    d = 1
    while d < rows:
        x = x + jnp.where(row >= d, pltpu.roll(x, d, axis=0), 0.0)
        d *= 2
    return x


def _ssd_chunk(xs, bs, cs, dt, a, states):
    chunk = xs.shape[0]
    acum = _row_cumsum(a)
    acum_t = acum.T
    bs_t = bs.T
    lane = lax.broadcasted_iota(jnp.int32, (1, LANE), 1)
    lo = lane < SSM_HEADDIM
    sub = lax.broadcasted_iota(jnp.int32, (LANE, 1), 0)
    ri = lax.broadcasted_iota(jnp.int32, (chunk, chunk), 0)
    ci = lax.broadcasted_iota(jnp.int32, (chunk, chunk), 1)
    causal = ri >= ci
    bs_b = bs.astype(BF16)

    ys = []
    new_states = []
    for g in range(2):
        h0, h1 = 2 * g, 2 * g + 1
        gmask = (lane >= g * SSM_STATE) & (lane < (g + 1) * SSM_STATE)
        csg = jnp.where(gmask, cs, 0.0).astype(BF16)
        gram = _dot_nt(csg, bs_b)
        sc = []
        dec = []
        for h in (h0, h1):
            col = acum[:, h:h + 1]
            rw = acum_t[h:h + 1, :]
            lmat = jnp.exp(jnp.where(causal, col - rw, -jnp.inf))
            sc.append((gram * lmat).astype(BF16))
            tot = acum[chunk - 1:chunk, h:h + 1]
            dec.append(jnp.exp(tot - rw))
        dtg = jnp.where(lo, dt[:, h0:h0 + 1], dt[:, h1:h1 + 1])
        xdt = xs[:, g * LANE:(g + 1) * LANE] * dtg
        rhs = jnp.concatenate([jnp.where(lo, xdt, 0.0), jnp.where(lo, 0.0, xdt)], axis=0).astype(BF16)
        y_diag = _dot(jnp.concatenate(sc, axis=1), rhs)
        st = states[g]
        eg = jnp.where(lo, jnp.exp(acum[:, h0:h0 + 1]), jnp.exp(acum[:, h1:h1 + 1]))
        y_off = _dot(csg, st.astype(BF16)) * eg
        bsg_t = jnp.where((sub >= g * SSM_STATE) & (sub < (g + 1) * SSM_STATE), bs_t, 0.0)
        lhs = jnp.concatenate([bsg_t * dec[0], bsg_t * dec[1]], axis=1).astype(BF16)
        etot = jnp.where(lo, jnp.exp(acum[chunk - 1:chunk, h0:h0 + 1]),
                         jnp.exp(acum[chunk - 1:chunk, h1:h1 + 1]))
        new_states.append(etot * st + _dot(lhs, rhs))
        ys.append(y_diag + y_off)
    return jnp.concatenate(ys, axis=1), new_states


def _ssd_kernel(u_ref, cw_ref, cb_ref, dtb_ref, alog_ref, dsk_ref, ng_ref, y_ref,
                xbuf_ref, st_ref, *, chunk):
    @pl.when(pl.program_id(1) == 0)
    def _():
        xbuf_ref[0:CONV_TAIL, :] = jnp.zeros((CONV_TAIL, SSM_XBC), F32)
        st_ref[...] = jnp.zeros(st_ref.shape, F32)

    u = u_ref[0]
    rows = u.shape[0]
    z = u[:, :SSM_INNER]
    xbc = _causal_conv(xbuf_ref, u[:, SSM_INNER:SSM_INNER + SSM_XBC], cw_ref, cb_ref, SSM_CONV)
    xbc = xbc * _sigmoid(xbc)
    xs = xbc[:, :SSM_INNER]
    bs = xbc[:, SSM_INNER:SSM_INNER + LANE]
    cs = xbc[:, SSM_INNER + LANE:]
    dt = _softplus(u[:, SSM_INNER + SSM_XBC:] + dtb_ref[...])
    a = dt * (-jnp.exp(alog_ref[...]))

    states = [st_ref[0], st_ref[1]]
    ys = []
    for c in range(rows // chunk):
        r = slice(c * chunk, (c + 1) * chunk)
        yc, states = _ssd_chunk(xs[r], bs[r], cs[r], dt[r], a[r], states)
        ys.append(yc)
    st_ref[0] = states[0]
    st_ref[1] = states[1]

    y = jnp.concatenate(ys, axis=0) + dsk_ref[...] * xs
    y = y * (z * _sigmoid(z))
    outs = []
    for g in range(2):
        yg = y[:, g * LANE:(g + 1) * LANE]
        outs.append(yg * lax.rsqrt(jnp.mean(yg * yg, axis=-1, keepdims=True) + EPS))
    y_ref[0] = jnp.concatenate(outs, axis=1) * ng_ref[...]


def _ssd(u_ssm, cw, cb, dtb, alog, dsk, ng, l, chunk, rows):
    b, s, _ = u_ssm.shape
    return pl.pallas_call(
        functools.partial(_ssd_kernel, chunk=chunk),
        out_shape=jax.ShapeDtypeStruct((b, s, SSM_INNER), F32),
        grid=(b, s // rows),
        in_specs=[pl.BlockSpec((1, rows, SSM_COLS), lambda i, j: (i, j, 0)),
                  _layer(cw, l), _layer(cb, l), _layer(dtb, l), _layer(alog, l),
                  _layer(dsk, l), _layer(ng, l)],
        out_specs=pl.BlockSpec((1, rows, SSM_INNER), lambda i, j: (i, j, 0)),
        scratch_shapes=[pltpu.VMEM((CONV_TAIL + rows, SSM_XBC), F32),
                        pltpu.VMEM((2, LANE, LANE), F32)],
        compiler_params=_cparams("parallel", "arbitrary"),
        name="ssd",
    )(u_ssm, cw, cb, dtb, alog, dsk, ng)


def _lru_kernel(u_ref, cw_ref, cb_ref, wa_ref, ba_ref, wi_ref, bi_ref, lam_ref, y_ref,
                xbuf_ref, h_ref, *, rows):
    @pl.when(pl.program_id(1) == 0)
    def _():
        xbuf_ref[0:CONV_TAIL, :] = jnp.zeros((CONV_TAIL, LRU_WIDTH), F32)
        h_ref[...] = jnp.zeros(h_ref.shape, F32)

    u = u_ref[0]
    gate = u[:, LRU_WIDTH:]
    xc = _causal_conv(xbuf_ref, u[:, :LRU_WIDTH], cw_ref, cb_ref, LRU_CONV)
    xcb = xc.astype(BF16)
    r = _sigmoid(_dot(xcb, wa_ref[...]) + ba_ref[...])
    i = _sigmoid(_dot(xcb, wi_ref[...]) + bi_ref[...])
    log_a = (-LRU_C) * r * _softplus(-lam_ref[...])
    a = jnp.exp(log_a)
    b = jnp.sqrt(-jnp.tanh(log_a) * (a * a + 1.0)) * (i * xc)
    row = lax.broadcasted_iota(jnp.int32, a.shape, 0)
    d = 1
    while d < rows:
        keep = row >= d
        a_prev = jnp.where(keep, pltpu.roll(a, d, axis=0), 1.0)
        b_prev = jnp.where(keep, pltpu.roll(b, d, axis=0), 0.0)
        b = a * b_prev + b
        a = a * a_prev
        d *= 2
    h = b + a * h_ref[...]
    h_ref[...] = h[rows - 1:rows, :]
    c0 = math.sqrt(2.0 / math.pi)
    gelu = 0.5 * gate * (1.0 + jnp.tanh(c0 * (gate + 0.044715 * (gate * gate * gate))))
    y_ref[0] = h * gelu


def _lru(u_lru, cw, cb, wa, ba, wi, bi, lam, l, rows):
    b, s, _ = u_lru.shape
    return pl.pallas_call(
        functools.partial(_lru_kernel, rows=rows),
        out_shape=jax.ShapeDtypeStruct((b, s, LRU_WIDTH), F32),
        grid=(b, s // rows),
        in_specs=[pl.BlockSpec((1, rows, LRU_COLS), lambda i, j: (i, j, 0)),
                  _layer(cw, l), _layer(cb, l), _layer(wa, l), _layer(ba, l),
                  _layer(wi, l), _layer(bi, l), _layer(lam, l)],
        out_specs=pl.BlockSpec((1, rows, LRU_WIDTH), lambda i, j: (i, j, 0)),
        scratch_shapes=[pltpu.VMEM((CONV_TAIL + rows, LRU_WIDTH), F32),
                        pltpu.VMEM((1, LRU_WIDTH), F32)],
        compiler_params=_cparams("parallel", "arbitrary"),
        name="rglru",
    )(u_lru, cw, cb, wa, ba, wi, bi, lam)


def _mix_xattn_kernel(ymla_ref, yssm_ref, ylru_ref, x_ref, gmla_ref, glru_ref, wout_ref,
                      gx_ref, wmq_ref, mk_ref, mv_ref, wmo_ref, o_ref):
    ymix = jnp.concatenate([_rms(ymla_ref[...], gmla_ref[...]), yssm_ref[...],
                            _rms(ylru_ref[...], glru_ref[...])], axis=1).astype(BF16)
    x1 = x_ref[...] + _dot(ymix, wout_ref[...])
    d = x1.shape[1]
    hd = d // MEM_HEADS
    hq = _rms(x1, gx_ref[...]).astype(BF16)
    q = (_dot(hq, wmq_ref[...]) * (1.0 / math.sqrt(hd))).astype(BF16)
    outs = []
    for h in range(MEM_HEADS):
        sl = slice(h * hd, (h + 1) * hd)
        s = _dot_nt(q[:, sl], mk_ref[0, :, sl])
        p = jnp.exp(s - jnp.max(s, axis=1, keepdims=True))
        l = jnp.sum(p, axis=1, keepdims=True)
        outs.append((_dot(p.astype(BF16), mv_ref[0, :, sl]) * (1.0 / l)).astype(BF16))
    o_ref[...] = x1 + _dot(jnp.concatenate(outs, axis=1), wmo_ref[...])


def _mix_xattn(ymla, yssm, ylru, x, gmla, glru, wout, gx, wmq, mk, mv, wmo, l, s, tm):
    t, d = x.shape
    ns = s // tm
    row = lambda c: pl.BlockSpec((tm, c), lambda i: (i, 0))
    mspec = pl.BlockSpec((None, 1, N_MEM, d), lambda i: (l, i // ns, 0, 0))
    return pl.pallas_call(
        _mix_xattn_kernel,
        out_shape=jax.ShapeDtypeStruct((t, d), F32),
        grid=(t // tm,),
        in_specs=[row(ymla.shape[1]), row(yssm.shape[1]), row(ylru.shape[1]), row(d),
                  _layer(gmla, l), _layer(glru, l), _layer(wout, l), _layer(gx, l),
                  _layer(wmq, l), mspec, mspec, _layer(wmo, l)],
        out_specs=row(d),
        compiler_params=_cparams("parallel"),
        name="mix_xattn",
    )(ymla, yssm, ylru, x, gmla, glru, wout, gx, wmq, mk, mv, wmo)


def _mlp_kernel(x_ref, g_ref, w1_ref, w2_ref, gf_ref, o_ref, *, ff_tile, final_norm):
    x = x_ref[...]
    h = _rms(x, g_ref[...]).astype(BF16)
    acc = x
    for j in range(w1_ref.shape[1] // ff_tile):
        sl = slice(j * ff_tile, (j + 1) * ff_tile)
        a = jnp.maximum(_dot(h, w1_ref[:, sl]), 0.0)
        acc = acc + _dot((a * a).astype(BF16), w2_ref[sl, :])
    o_ref[...] = _rms(acc, gf_ref[...]) if final_norm else acc


def _mlp(x, g, w1, w2, gf, l, tm, final_norm):
    t, d = x.shape
    row = pl.BlockSpec((tm, d), lambda i: (i, 0))
    return pl.pallas_call(
        functools.partial(_mlp_kernel, ff_tile=d, final_norm=final_norm),
        out_shape=jax.ShapeDtypeStruct((t, d), F32),
        grid=(t // tm,),
        in_specs=[row, _layer(g, l), _layer(w1, l, pipeline_mode=pl.Buffered(1)),
                  _layer(w2, l, pipeline_mode=pl.Buffered(1)), _full(gf.shape)],
        out_specs=row,
        compiler_params=_cparams("parallel"),
        name="mlp",
    )(x, g, w1, w2, gf)


def _pad_cols(w, n):
    return jnp.pad(w, [(0, 0)] * (w.ndim - 1) + [(0, n - w.shape[-1])])


def _pack_w_in(w):
    half = MLA_ROPE // 2
    o = 0
    cq = w[..., o:o + MLA_Q_RANK]; o += MLA_Q_RANK
    ckv = w[..., o:o + MLA_KV_RANK]; o += MLA_KV_RANK
    k1 = w[..., o:o + half]; k2 = w[..., o + half:o + MLA_ROPE]; o += MLA_ROPE
    ssm = w[..., o:o + 256 + SSM_XBC + SSM_HEADS]; o += 256 + SSM_XBC + SSM_HEADS
    lru = w[..., o:]
    ka = _pad_cols(jnp.concatenate([k1, k2], -1), LANE)
    kb = _pad_cols(jnp.concatenate([k2, k1], -1), LANE)
    return jnp.concatenate([cq, ckv, ka, kb, _pad_cols(ssm, SSM_COLS), lru], -1).astype(BF16)


def _rope_select():
    half = MLA_ROPE // 2
    p = np.zeros((2 * LANE, MLA_HEADS * LANE), np.float32)
    for h in range(MLA_HEADS):
        for f in range(half):
            p[h * half + f, h * LANE + f] = 1.0
            p[LANE + h * half + f, h * LANE + half + f] = 1.0
    return jnp.asarray(p, BF16)


def _block_diag(w):
    depth, n, d, e = w.shape
    eye = jnp.eye(n, dtype=w.dtype)
    return jnp.einsum('lnde,nm->lndme', w, eye).reshape(depth, n * d, n * e)


def _lane_vec(v, n=LANE):
    return _pad_cols(v, n)[:, None, :]


def kernel(x, mem, positions, mix_norm_g, w_in, mla_q_norm_g, mla_kv_norm_g, mla_w_uq, mla_w_ukv, mla_out_g, ssm_conv_w, ssm_conv_b, ssm_dt_bias, ssm_a_log, ssm_d, ssm_norm_g, lru_conv_w, lru_conv_b, lru_w_a, lru_b_a, lru_w_i, lru_b_i, lru_lambda, lru_out_g, w_out, xattn_norm_g, mem_norm_g, w_mq, w_mk, w_mv, w_mo, mlp_norm_g, w_mlp1, w_mlp2, final_norm_g):
    b, s, d = x.shape
    depth = w_in.shape[0]
    t = b * s
    tm = min(512, s)
    tq = min(256, s)
    tk = min(512, s)
    chunk = 128
    ssd_rows = min(1024, s)
    lru_rows = min(256, s)

    half = MLA_ROPE // 2
    wq = mla_w_uq.reshape(depth, MLA_Q_RANK, MLA_HEADS, MLA_NOPE + MLA_ROPE)
    wq_nope = wq[..., :MLA_NOPE].transpose(0, 2, 1, 3)
    wqr = jnp.concatenate([wq[..., MLA_NOPE:MLA_NOPE + half].reshape(depth, MLA_Q_RANK, -1),
                           wq[..., MLA_NOPE + half:].reshape(depth, MLA_Q_RANK, -1)], -1).astype(BF16)
    wkv = mla_w_ukv.reshape(depth, MLA_KV_RANK, MLA_HEADS, MLA_NOPE + MLA_V)
    wk = wkv[..., :MLA_NOPE].transpose(0, 2, 1, 3)
    wv = wkv[..., MLA_NOPE:].transpose(0, 2, 1, 3)
    zv = jnp.zeros_like(wv[:, 0::2])
    wuv = jnp.concatenate([jnp.concatenate([wv[:, 0::2], zv], -1),
                           jnp.concatenate([zv, wv[:, 1::2]], -1)], 2).astype(BF16)
    psel = _rope_select()

    w_in_p = _pack_w_in(w_in)
    wa_bd = _block_diag(lru_w_a).astype(BF16)
    wi_bd = _block_diag(lru_w_i).astype(BF16)
    ba = lru_b_a.reshape(depth, 1, LRU_WIDTH)
    bi = lru_b_i.reshape(depth, 1, LRU_WIDTH)
    dsk = jnp.repeat(ssm_d, SSM_HEADDIM, axis=1)[:, None, :]
    dtb = _lane_vec(ssm_dt_bias)
    alog = _lane_vec(ssm_a_log)

    bias = _causal_bias(tq, tk)
    cos_t, sin_t = _rope_tables(positions)
    memk, memv = _mem_kv(mem, mem_norm_g, w_mk.astype(BF16), w_mv.astype(BF16))
    wqa = _fold_q_absorb(wq_nope, wk)

    vec = lambda v: v[:, None, :]
    g_mix, g_q, g_kv = vec(mix_norm_g), vec(mla_q_norm_g), vec(mla_kv_norm_g)
    g_mla, g_lru, g_x, g_mlp = vec(mla_out_g), vec(lru_out_g), vec(xattn_norm_g), vec(mlp_norm_g)
    ssm_cb, ssm_ng = vec(ssm_conv_b), vec(ssm_norm_g)
    lru_cb, lam = vec(lru_conv_b), vec(lru_lambda)
    w_out_b, w_mq_b, w_mo_b = w_out.astype(BF16), w_mq.astype(BF16), w_mo.astype(BF16)
    w1_b, w2_b = w_mlp1.astype(BF16), w_mlp2.astype(BF16)

    xt = x.reshape(t, d)
    for l in range(depth):
        q, kc, u_ssm, u_lru = _in_proj(xt, g_mix, w_in_p, cos_t, sin_t, g_q, g_kv, wqa, wqr, psel,
                                       l, b, s, tm)
        y_mla = _attention(q, kc, wuv, bias, l, tq, tk).reshape(t, -1)
        y_ssm = _ssd(u_ssm.reshape(b, s, -1), ssm_conv_w, ssm_cb, dtb, alog, dsk, ssm_ng,
                     l, chunk, ssd_rows).reshape(t, -1)
        y_lru = _lru(u_lru.reshape(b, s, -1), lru_conv_w, lru_cb, wa_bd, ba, wi_bd, bi, lam,
                     l, lru_rows).reshape(t, -1)
        xt = _mix_xattn(y_mla, y_ssm, y_lru, xt, g_mla, g_lru, w_out_b, g_x, w_mq_b,
                        memk, memv, w_mo_b, l, s, tm)
        xt = _mlp(xt, g_mlp, w1_b, w2_b, final_norm_g[None], l, tm, final_norm=(l == depth - 1))
    return xt.reshape(b, s, d)
```

```python
import functools
import math

import numpy as np
import jax
import jax.numpy as jnp
from jax import lax
from jax.experimental import pallas as pl
from jax.experimental.pallas import tpu as pltpu

F32 = jnp.float32
BF16 = jnp.bfloat16

EPS = 1e-6
LANE = 128
CONV_TAIL = 8

MLA_HEADS = 8
MLA_NOPE = 64
MLA_ROPE = 32
MLA_V = 64
MLA_Q_RANK = 256
MLA_KV_RANK = 128
ROPE_THETA = 10000.0
MLA_KDIM = 2 * LANE

SSM_HEADS = 4
SSM_HEADDIM = 64
SSM_INNER = 256
SSM_STATE = 64
SSM_CONV = 4
SSM_XBC = 512
SSM_COLS = 256 + SSM_XBC + LANE

LRU_WIDTH = 256
LRU_BLOCKS = 4
LRU_CONV = 4
LRU_C = 8.0

MEM_HEADS = 4
N_MEM = 256

MLA_COLS = MLA_Q_RANK + MLA_KV_RANK + 2 * LANE
LRU_COLS = 2 * LRU_WIDTH

VMEM_LIMIT = 48 * 1024 * 1024
MLP_VMEM_LIMIT = 56 * 1024 * 1024


def _cparams(*sem):
    return pltpu.CompilerParams(dimension_semantics=sem, vmem_limit_bytes=VMEM_LIMIT)


def _rms(x, g):
    ms = jnp.mean(x * x, axis=-1, keepdims=True)
    return x * lax.rsqrt(ms + EPS) * g


def _sigmoid(x):
    return 1.0 / (1.0 + jnp.exp(-x))


def _softplus(x):
    return jnp.maximum(x, 0.0) + jnp.log1p(jnp.exp(-jnp.abs(x)))


def _dot(a, b):
    return jnp.dot(a, b, preferred_element_type=F32)


def _dot_nt(a, b):
    return lax.dot_general(a, b, (((1,), (1,)), ((), ())), preferred_element_type=F32)


def _full(shape):
    zeros = (0,) * len(shape)
    return pl.BlockSpec(shape, lambda *_: zeros)


def _layer(arr, l, **kw):
    idx = (l,) + (0,) * (arr.ndim - 1)
    return pl.BlockSpec((None,) + arr.shape[1:], lambda *_: idx, **kw)


def _rope_kernel(pos_ref, freq_ref, cos_ref, sin_ref):
    ang = pos_ref[...].astype(F32) * freq_ref[...]
    cos_ref[...] = jnp.cos(ang)
    sin_ref[...] = jnp.sin(ang)


def _rope_tables(positions):
    t = positions.size
    half = MLA_ROPE // 2
    per_row = LANE // half
    inv_freq = ROPE_THETA ** (-jnp.arange(half, dtype=F32) * 2.0 / MLA_ROPE)
    freq = jnp.tile(inv_freq, per_row).reshape(1, LANE)
    pos = jnp.repeat(positions.reshape(t // per_row, per_row), half, axis=1)
    out = jax.ShapeDtypeStruct(pos.shape, F32)
    cos_c, sin_c = pl.pallas_call(
        _rope_kernel,
        out_shape=(out, out),
        in_specs=[_full(pos.shape), _full((1, LANE))],
        out_specs=(_full(pos.shape),) * 2,
        compiler_params=_cparams(),
        name="rope_tables",
    )(pos, freq)
    expand = lambda a: jnp.tile(a.reshape(t, half), (1, per_row))
    return expand(cos_c), expand(sin_c)


def _mem_kv_kernel(mem_ref, g_ref, wk_ref, wv_ref, k_ref, v_ref):
    mn = _rms(mem_ref[0], g_ref[0]).astype(BF16)
    k_ref[0, 0] = _dot(mn, wk_ref[0]).astype(BF16)
    v_ref[0, 0] = _dot(mn, wv_ref[0]).astype(BF16)


def _mem_kv(mem, g, wk, wv):
    depth, d = g.shape
    b = mem.shape[0]
    out = jax.ShapeDtypeStruct((depth, b, N_MEM, d), BF16)
    wspec = pl.BlockSpec((1, d, d), lambda l, i: (l, 0, 0))
    ospec = pl.BlockSpec((1, 1, N_MEM, d), lambda l, i: (l, i, 0, 0))
    return pl.pallas_call(
        _mem_kv_kernel,
        out_shape=(out, out),
        grid=(depth, b),
        in_specs=[pl.BlockSpec((1, N_MEM, d), lambda l, i: (i, 0, 0)),
                  pl.BlockSpec((1, 1, d), lambda l, i: (l, 0, 0)), wspec, wspec],
        out_specs=(ospec, ospec),
        compiler_params=_cparams("parallel", "parallel"),
        name="mem_kv",
    )(mem, g.reshape(depth, 1, d), wk, wv)


def _fold_kernel(wq_ref, wk_ref, o_ref):
    for h in range(MLA_HEADS):
        o_ref[0, :, h * MLA_KV_RANK:(h + 1) * MLA_KV_RANK] = lax.dot_general(
            wq_ref[0, h], wk_ref[0, h], (((1,), (1,)), ((), ())),
            precision=lax.Precision.HIGHEST, preferred_element_type=F32).astype(BF16)


def _fold_q_absorb(wq_nope, wk):
    depth = wq_nope.shape[0]
    return pl.pallas_call(
        _fold_kernel,
        out_shape=jax.ShapeDtypeStruct((depth, MLA_Q_RANK, MLA_HEADS * MLA_KV_RANK), BF16),
        grid=(depth,),
        in_specs=[pl.BlockSpec((1, MLA_HEADS, MLA_Q_RANK, MLA_NOPE), lambda l: (l, 0, 0, 0)),
                  pl.BlockSpec((1, MLA_HEADS, MLA_KV_RANK, MLA_NOPE), lambda l: (l, 0, 0, 0))],
        out_specs=pl.BlockSpec((1, MLA_Q_RANK, MLA_HEADS * MLA_KV_RANK), lambda l: (l, 0, 0)),
        compiler_params=_cparams("parallel"),
        name="fold_q_absorb",
    )(wq_nope, wk)


def _in_proj_kernel(x_ref, g_ref, w_ref, cos_ref, sin_ref, gq_ref, gkv_ref, wqa_ref, wqr_ref, psel_ref,
                    q_ref, kc_ref, ssm_ref, lru_ref):
    h = _rms(x_ref[...], g_ref[...]).astype(BF16)
    u = _dot(h, w_ref[...])
    ssm_ref[...] = u[:, MLA_COLS:MLA_COLS + SSM_COLS]
    lru_ref[...] = u[:, MLA_COLS + SSM_COLS:]

    scale = math.log2(math.e) / math.sqrt(MLA_NOPE + MLA_ROPE)
    c = cos_ref[...]
    s = sin_ref[...]
    cqn = _rms(u[:, :MLA_Q_RANK], gq_ref[...]).astype(BF16)
    qlat = _dot(cqn, wqa_ref[...]) * scale
    qr = _dot(cqn, wqr_ref[...])
    r1 = qr[:, :LANE]
    r2 = qr[:, LANE:]
    roped = jnp.concatenate([r1 * c - r2 * s, r2 * c + r1 * s], axis=1) * scale
    qsel = _dot(roped.astype(BF16), psel_ref[...])
    for hd in range(MLA_HEADS):
        sl = slice(hd * LANE, (hd + 1) * LANE)
        q_ref[0, hd] = jnp.concatenate([qlat[:, sl], qsel[:, sl]], axis=1).astype(BF16)
    o = MLA_Q_RANK
    ckvn = _rms(u[:, o:o + MLA_KV_RANK], gkv_ref[...])
    lane = lax.broadcasted_iota(jnp.int32, (1, LANE), 1)
    sgn = jnp.where(lane % MLA_ROPE < MLA_ROPE // 2, -1.0, 1.0)
    o += MLA_KV_RANK
    kr = u[:, o:o + LANE] * c + u[:, o + LANE:o + 2 * LANE] * (s * sgn)
    kc_ref[0] = jnp.concatenate([ckvn, kr], axis=1).astype(BF16)


def _in_proj(x, g, w, cos_t, sin_t, gq, gkv, wqa, wqr, psel, l, b, s, tm):
    t, d = x.shape
    ns = s // tm
    row = lambda c: pl.BlockSpec((tm, c), lambda i: (i, 0))
    return pl.pallas_call(
        _in_proj_kernel,
        out_shape=(jax.ShapeDtypeStruct((b, MLA_HEADS, s, MLA_KDIM), BF16),
                   jax.ShapeDtypeStruct((b, s, MLA_KDIM), BF16),
                   jax.ShapeDtypeStruct((t, SSM_COLS), F32),
                   jax.ShapeDtypeStruct((t, LRU_COLS), F32)),
        grid=(t // tm,),
        in_specs=[row(d), _layer(g, l), _layer(w, l), row(LANE), row(LANE), _layer(gq, l), _layer(gkv, l),
                  _layer(wqa, l), _layer(wqr, l), _full(psel.shape)],
        out_specs=(pl.BlockSpec((1, MLA_HEADS, tm, MLA_KDIM), lambda i: (i // ns, 0, i % ns, 0)),
                   pl.BlockSpec((1, tm, MLA_KDIM), lambda i: (i // ns, i % ns, 0)),
                   row(SSM_COLS), row(LRU_COLS)),
        compiler_params=_cparams("parallel"),
        name="in_proj",
    )(x, g, w, cos_t, sin_t, gq, gkv, wqa, wqr, psel)


def _attn_kernel(q_ref, qn_ref, kc_ref, wuv_ref, bias_ref, y_ref, m_ref, acc_ref, s_ref, *, tq, tk):
    qi = pl.program_id(1)
    rows = MLA_HEADS * tq
    q = q_ref[0].reshape(rows, MLA_KDIM)
    m_ref[...] = jnp.full(m_ref.shape, -jnp.inf, F32)
    acc_ref[...] = jnp.zeros(acc_ref.shape, F32)
    ones = jnp.ones((tk, LANE), BF16)

    def ktile(j):
        return kc_ref[0, pl.ds(pl.multiple_of(j * tk, tk), tk), :]

    def scores(j):
        return _dot_nt(q, ktile(j))

    def absorb(slot, j, masked):
        if masked:
            bias = bias_ref[qi % (tk // tq)]
            load = lambda: (s_ref[slot].reshape(MLA_HEADS, tq, tk) + bias[None]).reshape(rows, tk)
        else:
            load = lambda: s_ref[slot]
        m_prev = m_ref[...]
        m_new = jnp.maximum(m_prev, jnp.max(load(), axis=1, keepdims=True))
        alpha = jnp.exp2(m_prev - m_new)
        p = jnp.exp2(load() - jnp.tile(m_new, (1, tk // LANE)))
        v = jnp.concatenate([ktile(j)[:, :MLA_KV_RANK], ones], axis=1)
        acc_ref[...] = jnp.tile(alpha, (1, 2)) * acc_ref[...] + _dot(p.astype(BF16), v)
        m_ref[...] = m_new

    n_full = (qi * tq) // tk

    @pl.when(qi == 0)
    def _():
        s_ref[0] = scores(0)

    def pair(i, carry):
        j = 2 * i
        s_ref[1] = scores(j + 1)
        absorb(0, j, False)
        s_ref[0] = scores(j + 2)
        absorb(1, j + 1, False)
        return carry

    lax.fori_loop(0, n_full // 2, pair, 0)
    odd = n_full % 2

    def finish(slot):
        absorb(slot, n_full, True)
        s_ref[0] = _dot_nt(qn_ref[0].reshape(rows, MLA_KDIM), ktile(0))

    @pl.when(odd == 1)
    def _():
        s_ref[1] = scores(n_full)
        absorb(0, n_full - 1, False)
        finish(1)

    @pl.when(odd == 0)
    def _():
        finish(0)

    acc = acc_ref[...]
    o = (acc[:, :MLA_KV_RANK] * (1.0 / acc[:, MLA_KV_RANK:])).astype(BF16)
    ys = []
    for pr in range(MLA_HEADS // 2):
        pair = jnp.concatenate([o[(2 * pr) * tq:(2 * pr + 1) * tq],
                                o[(2 * pr + 1) * tq:(2 * pr + 2) * tq]], axis=1)
        ys.append(_dot(pair, wuv_ref[pr]))
    y_ref[0] = jnp.concatenate(ys, axis=1)


def _causal_bias(tq, tk):
    shape = (tk // tq, tq, tk)
    o, r, c = (lax.broadcasted_iota(jnp.int32, shape, d) for d in range(3))
    return jnp.where(c <= r + o * tq, 0.0, -jnp.inf).astype(F32)


def _attention(q, kc, wuv, bias, l, tq, tk):
    b, h, s, kd = q.shape
    rows = h * tq
    last = s // tq - 1
    return pl.pallas_call(
        functools.partial(_attn_kernel, tq=tq, tk=tk),
        out_shape=jax.ShapeDtypeStruct((b, s, h * MLA_V), F32),
        grid=(b, s // tq),
        in_specs=[pl.BlockSpec((1, h, tq, kd), lambda i, j: (i, 0, j, 0)),
                  pl.BlockSpec((1, h, tq, kd), lambda i, j: (i, 0, jnp.minimum(j + 1, last), 0)),
                  pl.BlockSpec((1, s, kd), lambda i, j: (i, 0, 0)),
                  _layer(wuv, l), _full(bias.shape)],
        out_specs=pl.BlockSpec((1, tq, h * MLA_V), lambda i, j: (i, j, 0)),
        scratch_shapes=[pltpu.VMEM((rows, LANE), F32), pltpu.VMEM((rows, 2 * LANE), F32),
                        pltpu.VMEM((2, rows, tk), F32)],
        compiler_params=_cparams("parallel", "arbitrary"),
        name="mla_attention",
    )(q, q, kc, wuv, bias)


def _causal_conv(xbuf_ref, x, w_ref, b_ref, width):
    rows = x.shape[0]
    xbuf_ref[CONV_TAIL:CONV_TAIL + rows, :] = x
    y = b_ref[...]
    for k in range(width):
        y = y + w_ref[k:k + 1, :] * xbuf_ref[pl.ds(CONV_TAIL - width + 1 + k, rows), :]
    xbuf_ref[0:CONV_TAIL, :] = x[rows - CONV_TAIL:, :]
    return y


def _row_cumsum(x):
    rows = x.shape[0]
    row = lax.broadcasted_iota(jnp.int32, x.shape, 0)
    d = 1
    while d < rows:
        x = x + jnp.where(row >= d, pltpu.roll(x, d, axis=0), 0.0)
        d *= 2
    return x


def _ssd_chunk(xs, bs, cs, dt, a, states):
    chunk = xs.shape[0]
    acum = _row_cumsum(a)
    acum_t = acum.T
    bs_t = bs.T
    lane = lax.broadcasted_iota(jnp.int32, (1, LANE), 1)
    lo = lane < SSM_HEADDIM
    sub = lax.broadcasted_iota(jnp.int32, (LANE, 1), 0)
    ri = lax.broadcasted_iota(jnp.int32, (chunk, chunk), 0)
    ci = lax.broadcasted_iota(jnp.int32, (chunk, chunk), 1)
    causal = ri >= ci
    bs_b = bs.astype(BF16)

    ys = []
    new_states = []
    for g in range(2):
        h0, h1 = 2 * g, 2 * g + 1
        gmask = (lane >= g * SSM_STATE) & (lane < (g + 1) * SSM_STATE)
        csg = jnp.where(gmask, cs, 0.0).astype(BF16)
        gram = _dot_nt(csg, bs_b)
        sc = []
        dec = []
        for h in (h0, h1):
            col = acum[:, h:h + 1]
            rw = acum_t[h:h + 1, :]
            lmat = jnp.exp(jnp.where(causal, col - rw, -jnp.inf))
            sc.append((gram * lmat).astype(BF16))
            tot = acum[chunk - 1:chunk, h:h + 1]
            dec.append(jnp.exp(tot - rw))
        dtg = jnp.where(lo, dt[:, h0:h0 + 1], dt[:, h1:h1 + 1])
        xdt = xs[:, g * LANE:(g + 1) * LANE] * dtg
        rhs = jnp.concatenate([jnp.where(lo, xdt, 0.0), jnp.where(lo, 0.0, xdt)], axis=0).astype(BF16)
        y_diag = _dot(jnp.concatenate(sc, axis=1), rhs)
        st = states[g]
        eg = jnp.where(lo, jnp.exp(acum[:, h0:h0 + 1]), jnp.exp(acum[:, h1:h1 + 1]))
        y_off = _dot(csg, st.astype(BF16)) * eg
        bsg_t = jnp.where((sub >= g * SSM_STATE) & (sub < (g + 1) * SSM_STATE), bs_t, 0.0)
        lhs = jnp.concatenate([bsg_t * dec[0], bsg_t * dec[1]], axis=1).astype(BF16)
        etot = jnp.where(lo, jnp.exp(acum[chunk - 1:chunk, h0:h0 + 1]),
                         jnp.exp(acum[chunk - 1:chunk, h1:h1 + 1]))
        new_states.append(etot * st + _dot(lhs, rhs))
        ys.append(y_diag + y_off)
    return jnp.concatenate(ys, axis=1), new_states


def _ssd_kernel(u_ref, cw_ref, cb_ref, dtb_ref, alog_ref, dsk_ref, ng_ref, y_ref,
                xbuf_ref, st_ref, *, chunk):
    @pl.when(pl.program_id(1) == 0)
    def _():
        xbuf_ref[0:CONV_TAIL, :] = jnp.zeros((CONV_TAIL, SSM_XBC), F32)
        st_ref[...] = jnp.zeros(st_ref.shape, F32)

    u = u_ref[0]
    rows = u.shape[0]
    z = u[:, :SSM_INNER]
    xbc = _causal_conv(xbuf_ref, u[:, SSM_INNER:SSM_INNER + SSM_XBC], cw_ref, cb_ref, SSM_CONV)
    xbc = xbc * _sigmoid(xbc)
    xs = xbc[:, :SSM_INNER]
    bs = xbc[:, SSM_INNER:SSM_INNER + LANE]
    cs = xbc[:, SSM_INNER + LANE:]
    dt = _softplus(u[:, SSM_INNER + SSM_XBC:] + dtb_ref[...])
    a = dt * (-jnp.exp(alog_ref[...]))

    states = [st_ref[0], st_ref[1]]
    ys = []
    for c in range(rows // chunk):
        r = slice(c * chunk, (c + 1) * chunk)
        yc, states = _ssd_chunk(xs[r], bs[r], cs[r], dt[r], a[r], states)
        ys.append(yc)
    st_ref[0] = states[0]
    st_ref[1] = states[1]

    y = jnp.concatenate(ys, axis=0) + dsk_ref[...] * xs
    y = y * (z * _sigmoid(z))
    outs = []
    for g in range(2):
        yg = y[:, g * LANE:(g + 1) * LANE]
        outs.append(yg * lax.rsqrt(jnp.mean(yg * yg, axis=-1, keepdims=True) + EPS))
    y_ref[0] = jnp.concatenate(outs, axis=1) * ng_ref[...]


def _ssd(u_ssm, cw, cb, dtb, alog, dsk, ng, l, chunk, rows):
    b, s, _ = u_ssm.shape
    return pl.pallas_call(
        functools.partial(_ssd_kernel, chunk=chunk),
        out_shape=jax.ShapeDtypeStruct((b, s, SSM_INNER), F32),
        grid=(b, s // rows),
        in_specs=[pl.BlockSpec((1, rows, SSM_COLS), lambda i, j: (i, j, 0)),
                  _layer(cw, l), _layer(cb, l), _layer(dtb, l), _layer(alog, l),
                  _layer(dsk, l), _layer(ng, l)],
        out_specs=pl.BlockSpec((1, rows, SSM_INNER), lambda i, j: (i, j, 0)),
        scratch_shapes=[pltpu.VMEM((CONV_TAIL + rows, SSM_XBC), F32),
                        pltpu.VMEM((2, LANE, LANE), F32)],
        compiler_params=_cparams("parallel", "arbitrary"),
        name="ssd",
    )(u_ssm, cw, cb, dtb, alog, dsk, ng)


def _lru_kernel(u_ref, cw_ref, cb_ref, wa_ref, ba_ref, wi_ref, bi_ref, lam_ref, y_ref,
                xbuf_ref, h_ref, *, rows):
    @pl.when(pl.program_id(1) == 0)
    def _():
        xbuf_ref[0:CONV_TAIL, :] = jnp.zeros((CONV_TAIL, LRU_WIDTH), F32)
        h_ref[...] = jnp.zeros(h_ref.shape, F32)

    u = u_ref[0]
    gate = u[:, LRU_WIDTH:]
    xc = _causal_conv(xbuf_ref, u[:, :LRU_WIDTH], cw_ref, cb_ref, LRU_CONV)
    xcb = xc.astype(BF16)
    r = _sigmoid(_dot(xcb, wa_ref[...]) + ba_ref[...])
    i = _sigmoid(_dot(xcb, wi_ref[...]) + bi_ref[...])
    log_a = (-LRU_C) * r * _softplus(-lam_ref[...])
    a = jnp.exp(log_a)
    b = jnp.sqrt(-jnp.tanh(log_a) * (a * a + 1.0)) * (i * xc)
    row = lax.broadcasted_iota(jnp.int32, a.shape, 0)
    d = 1
    while d < rows:
        keep = row >= d
        a_prev = jnp.where(keep, pltpu.roll(a, d, axis=0), 1.0)
        b_prev = jnp.where(keep, pltpu.roll(b, d, axis=0), 0.0)
        b = a * b_prev + b
        a = a * a_prev
        d *= 2
    h = b + a * h_ref[...]
    h_ref[...] = h[rows - 1:rows, :]
    c0 = math.sqrt(2.0 / math.pi)
    gelu = 0.5 * gate * (1.0 + jnp.tanh(c0 * (gate + 0.044715 * (gate * gate * gate))))
    y_ref[0] = h * gelu


def _lru(u_lru, cw, cb, wa, ba, wi, bi, lam, l, rows):
    b, s, _ = u_lru.shape
    return pl.pallas_call(
        functools.partial(_lru_kernel, rows=rows),
        out_shape=jax.ShapeDtypeStruct((b, s, LRU_WIDTH), F32),
        grid=(b, s // rows),
        in_specs=[pl.BlockSpec((1, rows, LRU_COLS), lambda i, j: (i, j, 0)),
                  _layer(cw, l), _layer(cb, l), _layer(wa, l), _layer(ba, l),
                  _layer(wi, l), _layer(bi, l), _layer(lam, l)],
        out_specs=pl.BlockSpec((1, rows, LRU_WIDTH), lambda i, j: (i, j, 0)),
        scratch_shapes=[pltpu.VMEM((CONV_TAIL + rows, LRU_WIDTH), F32),
                        pltpu.VMEM((1, LRU_WIDTH), F32)],
        compiler_params=_cparams("parallel", "arbitrary"),
        name="rglru",
    )(u_lru, cw, cb, wa, ba, wi, bi, lam)


def _mix_xattn_kernel(ymla_ref, yssm_ref, ylru_ref, x_ref, gmla_ref, glru_ref, wout_ref,
                      gx_ref, wmq_ref, mk_ref, mv_ref, wmo_ref, o_ref):
    ymix = jnp.concatenate([_rms(ymla_ref[...], gmla_ref[...]), yssm_ref[...],
                            _rms(ylru_ref[...], glru_ref[...])], axis=1).astype(BF16)
    x1 = x_ref[...] + _dot(ymix, wout_ref[...])
    d = x1.shape[1]
    hd = d // MEM_HEADS
    hq = _rms(x1, gx_ref[...]).astype(BF16)
    q = (_dot(hq, wmq_ref[...]) * (1.0 / math.sqrt(hd))).astype(BF16)
    outs = []
    for h in range(MEM_HEADS):
        sl = slice(h * hd, (h + 1) * hd)
        s = _dot_nt(q[:, sl], mk_ref[0, :, sl])
        p = jnp.exp(s - jnp.max(s, axis=1, keepdims=True))
        l = jnp.sum(p, axis=1, keepdims=True)
        outs.append((_dot(p.astype(BF16), mv_ref[0, :, sl]) * (1.0 / l)).astype(BF16))
    o_ref[...] = x1 + _dot(jnp.concatenate(outs, axis=1), wmo_ref[...])


def _mix_xattn(ymla, yssm, ylru, x, gmla, glru, wout, gx, wmq, mk, mv, wmo, l, s, tm):
    t, d = x.shape
    ns = s // tm
    row = lambda c: pl.BlockSpec((tm, c), lambda i: (i, 0))
    mspec = pl.BlockSpec((None, 1, N_MEM, d), lambda i: (l, i // ns, 0, 0))
    return pl.pallas_call(
        _mix_xattn_kernel,
        out_shape=jax.ShapeDtypeStruct((t, d), F32),
        grid=(t // tm,),
        in_specs=[row(ymla.shape[1]), row(yssm.shape[1]), row(ylru.shape[1]), row(d),
                  _layer(gmla, l), _layer(glru, l), _layer(wout, l), _layer(gx, l),
                  _layer(wmq, l), mspec, mspec, _layer(wmo, l)],
        out_specs=row(d),
        compiler_params=_cparams("parallel"),
        name="mix_xattn",
    )(ymla, yssm, ylru, x, gmla, glru, wout, gx, wmq, mk, mv, wmo)


def _mlp_kernel(x_ref, g_ref, w1_ref, w2_ref, gf_ref, o_ref, *, ff_tile, final_norm):
    x = x_ref[...]
    h = _rms(x, g_ref[...]).astype(BF16)
    acc = x
    for j in range(w1_ref.shape[1] // ff_tile):
        sl = slice(j * ff_tile, (j + 1) * ff_tile)
        a = jnp.maximum(_dot(h, w1_ref[:, sl].astype(BF16)), 0.0)
        acc = acc + _dot((a * a).astype(BF16), w2_ref[sl, :].astype(BF16))
    o_ref[...] = _rms(acc, gf_ref[...]) if final_norm else acc


def _mlp(x, g, w1, w2, gf, l, tm, final_norm):
    t, d = x.shape
    row = pl.BlockSpec((tm, d), lambda i: (i, 0))
    return pl.pallas_call(
        functools.partial(_mlp_kernel, ff_tile=d, final_norm=final_norm),
        out_shape=jax.ShapeDtypeStruct((t, d), F32),
        grid=(t // tm,),
        in_specs=[row, _layer(g, l), _layer(w1, l, pipeline_mode=pl.Buffered(1)),
                  _layer(w2, l, pipeline_mode=pl.Buffered(1)), _full(gf.shape)],
        out_specs=row,
        compiler_params=pltpu.CompilerParams(dimension_semantics=("parallel",), vmem_limit_bytes=MLP_VMEM_LIMIT),
        name="mlp",
    )(x, g, w1, w2, gf)


def _pad_cols(w, n):
    return jnp.pad(w, [(0, 0)] * (w.ndim - 1) + [(0, n - w.shape[-1])])


def _pack_w_in(w):
    half = MLA_ROPE // 2
    o = 0
    cq = w[..., o:o + MLA_Q_RANK]; o += MLA_Q_RANK
    ckv = w[..., o:o + MLA_KV_RANK]; o += MLA_KV_RANK
    k1 = w[..., o:o + half]; k2 = w[..., o + half:o + MLA_ROPE]; o += MLA_ROPE
    ssm = w[..., o:o + 256 + SSM_XBC + SSM_HEADS]; o += 256 + SSM_XBC + SSM_HEADS
    lru = w[..., o:]
    ka = _pad_cols(jnp.concatenate([k1, k2], -1), LANE)
    kb = _pad_cols(jnp.concatenate([k2, k1], -1), LANE)
    return jnp.concatenate([cq, ckv, ka, kb, _pad_cols(ssm, SSM_COLS), lru], -1).astype(BF16)


def _rope_select():
    half = MLA_ROPE // 2
    p = np.zeros((2 * LANE, MLA_HEADS * LANE), np.float32)
    for h in range(MLA_HEADS):
        for f in range(half):
            p[h * half + f, h * LANE + f] = 1.0
            p[LANE + h * half + f, h * LANE + half + f] = 1.0
    return jnp.asarray(p, BF16)


def _block_diag(w):
    depth, n, d, e = w.shape
    eye = jnp.eye(n, dtype=w.dtype)
    return jnp.einsum('lnde,nm->lndme', w, eye).reshape(depth, n * d, n * e)


def _lane_vec(v, n=LANE):
    return _pad_cols(v, n)[:, None, :]


def kernel(x, mem, positions, mix_norm_g, w_in, mla_q_norm_g, mla_kv_norm_g, mla_w_uq, mla_w_ukv, mla_out_g, ssm_conv_w, ssm_conv_b, ssm_dt_bias, ssm_a_log, ssm_d, ssm_norm_g, lru_conv_w, lru_conv_b, lru_w_a, lru_b_a, lru_w_i, lru_b_i, lru_lambda, lru_out_g, w_out, xattn_norm_g, mem_norm_g, w_mq, w_mk, w_mv, w_mo, mlp_norm_g, w_mlp1, w_mlp2, final_norm_g):
    b, s, d = x.shape
    depth = w_in.shape[0]
    t = b * s
    tm = min(512, s)
    tq = min(256, s)
    tk = min(512, s)
    chunk = 128
    ssd_rows = min(1024, s)
    lru_rows = min(256, s)

    half = MLA_ROPE // 2
    wq = mla_w_uq.reshape(depth, MLA_Q_RANK, MLA_HEADS, MLA_NOPE + MLA_ROPE)
    wq_nope = wq[..., :MLA_NOPE].transpose(0, 2, 1, 3)
    wqr = jnp.concatenate([wq[..., MLA_NOPE:MLA_NOPE + half].reshape(depth, MLA_Q_RANK, -1),
                           wq[..., MLA_NOPE + half:].reshape(depth, MLA_Q_RANK, -1)], -1).astype(BF16)
    wkv = mla_w_ukv.reshape(depth, MLA_KV_RANK, MLA_HEADS, MLA_NOPE + MLA_V)
    wk = wkv[..., :MLA_NOPE].transpose(0, 2, 1, 3)
    wv = wkv[..., MLA_NOPE:].transpose(0, 2, 1, 3)
    zv = jnp.zeros_like(wv[:, 0::2])
    wuv = jnp.concatenate([jnp.concatenate([wv[:, 0::2], zv], -1),
                           jnp.concatenate([zv, wv[:, 1::2]], -1)], 2).astype(BF16)
    psel = _rope_select()

    w_in_p = _pack_w_in(w_in)
    wa_bd = _block_diag(lru_w_a).astype(BF16)
    wi_bd = _block_diag(lru_w_i).astype(BF16)
    ba = lru_b_a.reshape(depth, 1, LRU_WIDTH)
    bi = lru_b_i.reshape(depth, 1, LRU_WIDTH)
    dsk = jnp.repeat(ssm_d, SSM_HEADDIM, axis=1)[:, None, :]
    dtb = _lane_vec(ssm_dt_bias)
    alog = _lane_vec(ssm_a_log)

    bias = _causal_bias(tq, tk)
    cos_t, sin_t = _rope_tables(positions)
    memk, memv = _mem_kv(mem, mem_norm_g, w_mk.astype(BF16), w_mv.astype(BF16))
    wqa = _fold_q_absorb(wq_nope, wk)

    vec = lambda v: v[:, None, :]
    g_mix, g_q, g_kv = vec(mix_norm_g), vec(mla_q_norm_g), vec(mla_kv_norm_g)
    g_mla, g_lru, g_x, g_mlp = vec(mla_out_g), vec(lru_out_g), vec(xattn_norm_g), vec(mlp_norm_g)
    ssm_cb, ssm_ng = vec(ssm_conv_b), vec(ssm_norm_g)
    lru_cb, lam = vec(lru_conv_b), vec(lru_lambda)
    w_out_b, w_mq_b, w_mo_b = w_out.astype(BF16), w_mq.astype(BF16), w_mo.astype(BF16)

    xt = x.reshape(t, d)
    for l in range(depth):
        q, kc, u_ssm, u_lru = _in_proj(xt, g_mix, w_in_p, cos_t, sin_t, g_q, g_kv, wqa, wqr, psel,
                                       l, b, s, tm)
        y_mla = _attention(q, kc, wuv, bias, l, tq, tk).reshape(t, -1)
        y_ssm = _ssd(u_ssm.reshape(b, s, -1), ssm_conv_w, ssm_cb, dtb, alog, dsk, ssm_ng,
                     l, chunk, ssd_rows).reshape(t, -1)
        y_lru = _lru(u_lru.reshape(b, s, -1), lru_conv_w, lru_cb, wa_bd, ba, wi_bd, bi, lam,
                     l, lru_rows).reshape(t, -1)
        xt = _mix_xattn(y_mla, y_ssm, y_lru, xt, g_mla, g_lru, w_out_b, g_x, w_mq_b,
                        memk, memv, w_mo_b, l, s, tm)
        xt = _mlp(xt, g_mlp, w_mlp1, w_mlp2, final_norm_g[None], l, tm, final_norm=(l == depth - 1))
    return xt.reshape(b, s, d)
```
